```python
import jax, jax.numpy as jnp
from jax import lax
import numpy as np

D_MODEL = 1024
BATCH = 8
SEQ = 4096
DEPTH = 2

N_BRANCH = 3
BRANCH_W = D_MODEL // 2
NORM_EPS = 1e-6
MAX_POS_OFFSET = 1024
MLA_HEADS = 8
MLA_NOPE = 64
MLA_ROPE = 32
MLA_V = BRANCH_W // MLA_HEADS
MLA_Q_LORA = 256
MLA_KV_LORA = 128
ROPE_BASE = 10000.0
Q_BLOCK = 128
RWKV_HEADS = 8
RWKV_HEAD = BRANCH_W // RWKV_HEADS
RWKV_DECAY_RANK = 64
RWKV_ICLR_RANK = 64
RWKV_IN = 3 * BRANCH_W + RWKV_DECAY_RANK + RWKV_ICLR_RANK
RWKV_GN_EPS = 64e-5
GLA_HEADS = 4
GLA_DK = 64
GLA_DV = BRANCH_W // GLA_HEADS
GLA_GATE_RANK = 16
GLA_TAU = 16.0
GLA_CHUNK = 64
IN_SIZES = (MLA_Q_LORA, MLA_KV_LORA, MLA_ROPE, RWKV_IN,
            GLA_HEADS * GLA_DK, GLA_HEADS * GLA_DK, GLA_HEADS * GLA_DV, GLA_GATE_RANK,
            N_BRANCH * BRANCH_W, N_BRANCH * D_MODEL)
N_IN = sum(IN_SIZES)

kernel_name = "hybrid_mla_rwkv7_gla_gated_merge"


def rms_norm(x, g, eps=NORM_EPS):
    xf = x.astype(jnp.float32)
    y = xf * lax.rsqrt(jnp.mean(xf * xf, axis=-1, keepdims=True) + eps)
    return (y * g.astype(jnp.float32)).astype(x.dtype)


def rope_tables(positions):
    inv = 1.0 / (ROPE_BASE ** (jnp.arange(0, MLA_ROPE, 2, dtype=jnp.float32) / MLA_ROPE))
    ang = positions.astype(jnp.float32)[..., None] * inv
    return jnp.cos(ang), jnp.sin(ang)


def apply_rope(x, cos, sin):
    xf = x.astype(jnp.float32)
    x1, x2 = xf[..., : MLA_ROPE // 2], xf[..., MLA_ROPE // 2:]
    return jnp.concatenate([x1 * cos - x2 * sin, x2 * cos + x1 * sin], axis=-1).astype(x.dtype)


def mla_branch(c_q, c_kv, k_rope, cos, sin, q_norm, kv_norm, w_uq, w_ukv):
    B, S, _ = c_q.shape
    q = (rms_norm(c_q, q_norm) @ w_uq).reshape(B, S, MLA_HEADS, MLA_NOPE + MLA_ROPE)
    q_nope = q[..., :MLA_NOPE]
    q_rope = apply_rope(q[..., MLA_NOPE:], cos[:, :, None, :], sin[:, :, None, :])
    kv = (rms_norm(c_kv, kv_norm) @ w_ukv).reshape(B, S, MLA_HEADS, MLA_NOPE + MLA_V)
    k_nope, v = kv[..., :MLA_NOPE], kv[..., MLA_NOPE:]
    k_r = apply_rope(k_rope, cos, sin)
    scale = (MLA_NOPE + MLA_ROPE) ** -0.5
    n_blk = S // Q_BLOCK
    qn_b = q_nope.reshape(B, n_blk, Q_BLOCK, MLA_HEADS, MLA_NOPE).transpose(1, 0, 2, 3, 4)
    qr_b = q_rope.reshape(B, n_blk, Q_BLOCK, MLA_HEADS, MLA_ROPE).transpose(1, 0, 2, 3, 4)
    k_idx = jnp.arange(S)

    def block(args):
        qn, qr, i = args
        s = (jnp.einsum('bqhd,bkhd->bhqk', qn, k_nope)
             + jnp.einsum('bqhr,bkr->bhqk', qr, k_r)).astype(jnp.float32) * scale
        q_idx = i * Q_BLOCK + jnp.arange(Q_BLOCK)
        s = jnp.where(k_idx[None, :] <= q_idx[:, None], s, -jnp.inf)
        p = jax.nn.softmax(s, axis=-1).astype(v.dtype)
        return jnp.einsum('bhqk,bkhd->bqhd', p, v)

    o = lax.map(block, (qn_b, qr_b, jnp.arange(n_blk)))
    return o.transpose(1, 0, 2, 3, 4).reshape(B, S, MLA_HEADS * MLA_V)


def rwkv7_branch(u, mu, w0, w_up, a0, a_up, k_k, k_a, r_k, ln_w, ln_b):
    B, S, _ = u.shape
    W = BRANCH_W
    u_prev = jnp.pad(u[:, :-1], ((0, 0), (1, 0), (0, 0)))
    u = u + (u_prev - u) * mu
    r, k, v, lw, la = jnp.split(u, [W, 2 * W, 3 * W, 3 * W + RWKV_DECAY_RANK], axis=-1)
    w = -jax.nn.softplus(-(w0 + jnp.tanh(lw) @ w_up)) - 0.5
    decay = jnp.exp(-jnp.exp(w.astype(jnp.float32)))
    a = jax.nn.sigmoid(a0 + la @ a_up)
    heads = lambda t: t.astype(jnp.float32).reshape(B, S, RWKV_HEADS, RWKV_HEAD)
    kk = heads(k * k_k)
    kk = kk / jnp.maximum(jnp.sqrt(jnp.sum(kk * kk, axis=-1, keepdims=True)), 1e-12)
    k = k * (1.0 + (a - 1.0) * k_a)
    r_h, k_h, v_h, a_h, w_h = heads(r), heads(k), heads(v), heads(a), heads(decay)

    def step(state, inp):
        r_t, w_t, k_t, v_t, kk_t, a_t = inp
        sa = jnp.einsum('bhvk,bhk->bhv', state, -kk_t)
        state = (state * w_t[:, :, None, :] + sa[..., None] * (kk_t * a_t)[:, :, None, :]
                 + v_t[..., None] * k_t[:, :, None, :])
        return state, jnp.einsum('bhvk,bhk->bhv', state, r_t)

    xs = tuple(t.transpose(1, 0, 2, 3) for t in (r_h, w_h, k_h, v_h, kk, a_h))
    state0 = jnp.zeros((B, RWKV_HEADS, RWKV_HEAD, RWKV_HEAD), jnp.float32)
    _, y = lax.scan(step, state0, xs)
    y = y.transpose(1, 0, 2, 3)
    mean = jnp.mean(y, axis=-1, keepdims=True)
    var = jnp.mean(jnp.square(y - mean), axis=-1, keepdims=True)
    y = ((y - mean) * lax.rsqrt(var + RWKV_GN_EPS)).reshape(B, S, W)
    y = y * ln_w.astype(jnp.float32) + ln_b.astype(jnp.float32)
    bonus = jnp.sum(r_h * k_h * r_k.astype(jnp.float32), axis=-1, keepdims=True) * v_h
    return (y + bonus.reshape(B, S, W)).astype(u.dtype)


def gla_branch(q, k, v, lat, a_up, a_b, g_norm):
    B, S, _ = q.shape
    H, C = GLA_HEADS, GLA_CHUNK
    n_c = S // C
    log_a = jax.nn.log_sigmoid((lat @ a_up + a_b).astype(jnp.float32)) / GLA_TAU

    def chunks(t, d):
        return t.astype(jnp.float32).reshape(B, n_c, C, H, d).transpose(1, 0, 2, 3, 4)

    qc = chunks(q * GLA_DK ** -0.5, GLA_DK)
    kc = chunks(k, GLA_DK)
    vc = chunks(v, GLA_DV)
    bc = jnp.cumsum(chunks(log_a, GLA_DK), axis=2)
    causal = jnp.tril(jnp.ones((C, C), dtype=bool))[None, :, :, None, None]

    def step(state, inp):
        q_t, k_t, v_t, b_t = inp
        diff = b_t[:, :, None] - b_t[:, None, :]
        dec = jnp.exp(jnp.where(causal, diff, -jnp.inf))
        att = jnp.einsum('bihd,bjhd,bijhd->bhij', q_t, k_t, dec)
        o = (jnp.einsum('bhij,bjhv->bihv', att, v_t)
             + jnp.einsum('bihk,bhkv->bihv', q_t * jnp.exp(b_t), state))
        b_last = b_t[:, -1]
        state = (state * jnp.exp(b_last)[..., None]
                 + jnp.einsum('bjhk,bjhv->bhkv', k_t * jnp.exp(b_last[:, None] - b_t), v_t))
        return state, o

    state0 = jnp.zeros((B, H, GLA_DK, GLA_DV), jnp.float32)
    _, o = lax.scan(step, state0, (qc, kc, vc, bc))
    o = o.transpose(1, 0, 2, 3, 4).reshape(B, S, H, GLA_DV)
    o = rms_norm(o, g_norm.reshape(H, GLA_DV))
    return o.reshape(B, S, H * GLA_DV).astype(q.dtype)


def setup_inputs(seed: int = 0) -> dict:
    key = jax.random.key(seed)
    ks = jax.random.split(key, 24)
    L, W = DEPTH, BRANCH_W

    def dense(k, shape, fan_in, scale=1.0):
        return scale * fan_in ** -0.5 * jax.random.normal(k, shape, jnp.float32)

    def gain(k, shape):
        return 1.0 + 0.05 * jax.random.normal(k, shape, jnp.float32)

    x = jax.random.normal(ks[0], (BATCH, SEQ, D_MODEL), jnp.float32)
    offset = jax.random.randint(ks[1], (BATCH, 1), 0, MAX_POS_OFFSET, dtype=jnp.int32)
    positions = offset + jnp.arange(SEQ, dtype=jnp.int32)[None, :]
    return {
        "x": x,
        "positions": positions,
        "norm_pre": gain(ks[2], (L, D_MODEL)),
        "w_in": dense(ks[3], (L, D_MODEL, N_IN), D_MODEL),
        "mla_q_norm": gain(ks[4], (L, MLA_Q_LORA)),
        "mla_kv_norm": gain(ks[5], (L, MLA_KV_LORA)),
        "mla_w_uq": dense(ks[6], (L, MLA_Q_LORA, MLA_HEADS * (MLA_NOPE + MLA_ROPE)), MLA_Q_LORA),
        "mla_w_ukv": dense(ks[7], (L, MLA_KV_LORA, MLA_HEADS * (MLA_NOPE + MLA_V)), MLA_KV_LORA),
        "rwkv_mu": jax.random.uniform(ks[8], (L, RWKV_IN), jnp.float32),
        "rwkv_w0": jax.random.uniform(ks[9], (L, W), jnp.float32, -6.0, -1.0),
        "rwkv_w_up": dense(ks[10], (L, RWKV_DECAY_RANK, W), RWKV_DECAY_RANK, 0.5),
        "rwkv_a0": 0.1 * jax.random.normal(ks[11], (L, W), jnp.float32),
        "rwkv_a_up": dense(ks[12], (L, RWKV_ICLR_RANK, W), RWKV_ICLR_RANK, 0.5),
        "rwkv_k_k": 0.85 + 0.05 * jax.random.normal(ks[13], (L, W), jnp.float32),
        "rwkv_k_a": 1.0 + 0.05 * jax.random.normal(ks[14], (L, W), jnp.float32),
        "rwkv_r_k": 0.1 * jax.random.normal(ks[15], (L, RWKV_HEADS, RWKV_HEAD), jnp.float32),
        "rwkv_ln_w": gain(ks[16], (L, W)),
        "rwkv_ln_b": 0.01 * jax.random.normal(ks[17], (L, W), jnp.float32),
        "gla_a_up": dense(ks[18], (L, GLA_GATE_RANK, GLA_HEADS * GLA_DK), GLA_GATE_RANK),
        "gla_a_b": 0.1 * jax.random.normal(ks[19], (L, GLA_HEADS * GLA_DK), jnp.float32),
        "gla_norm": gain(ks[20], (L, W)),
        "w_branch_out": dense(ks[21], (L, N_BRANCH, W, D_MODEL), W),
        "w_out": dense(ks[22], (L, D_MODEL, D_MODEL), D_MODEL),
        "norm_post": gain(ks[23], (L, D_MODEL)),
    }


def reference(x, positions, norm_pre, w_in, mla_q_norm, mla_kv_norm, mla_w_uq, mla_w_ukv,
              rwkv_mu, rwkv_w0, rwkv_w_up, rwkv_a0, rwkv_a_up, rwkv_k_k, rwkv_k_a, rwkv_r_k,
              rwkv_ln_w, rwkv_ln_b, gla_a_up, gla_a_b, gla_norm, w_branch_out, w_out, norm_post):
    B, S, _ = x.shape
    splits = [int(i) for i in np.cumsum(IN_SIZES)[:-1]]
    cos, sin = rope_tables(positions)
    for l in range(DEPTH):
        h = rms_norm(x, norm_pre[l])
        proj = h @ w_in[l]
        (c_q, c_kv, k_rope, u_rwkv, g_q, g_k, g_v, g_lat,
         br_gate, merge_gate) = jnp.split(proj, splits, axis=-1)
        y_mla = mla_branch(c_q, c_kv, k_rope, cos, sin, mla_q_norm[l], mla_kv_norm[l],
                           mla_w_uq[l], mla_w_ukv[l])
        y_rwkv = rwkv7_branch(u_rwkv, rwkv_mu[l], rwkv_w0[l], rwkv_w_up[l], rwkv_a0[l],
                              rwkv_a_up[l], rwkv_k_k[l], rwkv_k_a[l], rwkv_r_k[l],
                              rwkv_ln_w[l], rwkv_ln_b[l])
        y_gla = gla_branch(g_q, g_k, g_v, g_lat, gla_a_up[l], gla_a_b[l], gla_norm[l])
        ys = (jnp.stack([y_mla, y_rwkv, y_gla], axis=2)
              * jax.nn.silu(br_gate).reshape(B, S, N_BRANCH, BRANCH_W))
        branch = jnp.einsum('bsnw,nwd->bsnd', ys, w_branch_out[l])
        merged = jnp.sum(jax.nn.sigmoid(merge_gate).reshape(B, S, N_BRANCH, D_MODEL) * branch, axis=2)
        x = x + rms_norm(merged @ w_out[l], norm_post[l])
    return x
```

```python
import functools

import jax
import jax.numpy as jnp
from jax import lax
from jax.experimental import pallas as pl
from jax.experimental.pallas import tpu as pltpu

F32 = jnp.float32
BF16 = jnp.bfloat16

D_MODEL = 1024
N_BRANCH = 3
BRANCH_W = D_MODEL // 2
NORM_EPS = 1e-6
MLA_HEADS = 8
MLA_NOPE = 64
MLA_ROPE = 32
MLA_V = BRANCH_W // MLA_HEADS
MLA_Q_LORA = 256
MLA_KV_LORA = 128
ROPE_BASE = 10000.0
RWKV_HEADS = 8
RWKV_HEAD = BRANCH_W // RWKV_HEADS
RWKV_DECAY_RANK = 64
RWKV_ICLR_RANK = 64
RWKV_IN = 3 * BRANCH_W + RWKV_DECAY_RANK + RWKV_ICLR_RANK
RWKV_GN_EPS = 64e-5
GLA_HEADS = 4
GLA_DK = 64
GLA_DV = BRANCH_W // GLA_HEADS
GLA_GATE_RANK = 16
GLA_TAU = 16.0
IN_SIZES = (MLA_Q_LORA, MLA_KV_LORA, MLA_ROPE, RWKV_IN,
            GLA_HEADS * GLA_DK, GLA_HEADS * GLA_DK, GLA_HEADS * GLA_DV, GLA_GATE_RANK,
            N_BRANCH * BRANCH_W, N_BRANCH * D_MODEL)

LANE = 128
MLA_IN_W = 640
GLA_IN_W = 1152
GATE_W = N_BRANCH * BRANCH_W
MERGE_W = N_BRANCH * D_MODEL
PROJ_W = MLA_IN_W + RWKV_IN + GLA_IN_W + GATE_W + MERGE_W
QK_W = 128

RWKV_CHUNK = 64
RWKV_QUAD = 256
GLA_BLOCK = 16
VMEM_LIMIT = 48 * 1024 * 1024


def _dot(a, b):
    return jnp.dot(a, b, preferred_element_type=F32)


def _dot_nt(a, b):
    return lax.dot_general(a, b, (((1,), (1,)), ((), ())), preferred_element_type=F32)


def _dot_tn(a, b):
    return lax.dot_general(a, b, (((0,), (0,)), ((), ())), preferred_element_type=F32)


def _split(x, parts):
    out = []
    rem = x
    for _ in range(parts):
        t = rem.astype(BF16)
        out.append(t)
        rem = rem - t.astype(F32)
    return out


def _dot_exact_rhs(a, b_bf16, parts=3):
    acc = None
    for t in _split(a, parts):
        d = _dot(t, b_bf16)
        acc = d if acc is None else acc + d
    return acc


def _dot_exact_lhs(a_bf16, b, parts=3):
    acc = None
    for t in _split(b, parts):
        d = _dot(a_bf16, t)
        acc = d if acc is None else acc + d
    return acc


def _dot_hi(a, b):
    a1, a2 = _split(a, 2)
    b1, b2 = _split(b, 2)
    return _dot(a1, b1) + (_dot(a1, b2) + _dot(a2, b1))


def _rms(x, g):
    return x * lax.rsqrt(jnp.mean(x * x, axis=-1, keepdims=True) + NORM_EPS) * g


def _rope_kernel(pos_ref, inv_ref, cos_ref, sin_ref):
    ang = pos_ref[...] * inv_ref[...]
    lane = lax.broadcasted_iota(jnp.int32, ang.shape, 1)
    rope = (lane >= MLA_NOPE) & (lane < MLA_NOPE + MLA_ROPE)
    cos_ref[...] = jnp.where(rope, jnp.cos(ang), jnp.where(lane < MLA_NOPE, 1.0, 0.0))
    sin_ref[...] = jnp.where(rope, jnp.sin(ang), 0.0)


def _rope_tables(positions, ts):
    B, S = positions.shape
    inv = 1.0 / (ROPE_BASE ** (jnp.arange(0, MLA_ROPE, 2, dtype=F32) / MLA_ROPE))
    inv_row = jnp.zeros((1, QK_W), F32).at[0, MLA_NOPE:MLA_NOPE + MLA_ROPE].set(jnp.tile(inv, 2))
    pos = positions.astype(F32)[..., None]
    return pl.pallas_call(
        _rope_kernel,
        grid=(B, S // ts),
        in_specs=[pl.BlockSpec((None, ts, 1), lambda b, i: (b, i, 0)),
                  pl.BlockSpec((1, QK_W), lambda b, i: (0, 0))],
        out_specs=[pl.BlockSpec((None, ts, QK_W), lambda b, i: (b, i, 0))] * 2,
        out_shape=[jax.ShapeDtypeStruct((B, S, QK_W), F32)] * 2,
        name="rope_tables",
    )(pos, inv_row)


_PROJ_SPLITS = (MLA_IN_W, RWKV_IN, GLA_IN_W, GATE_W, MERGE_W)


def _inproj_kernel(x_ref, g_ref, w_ref, mla_ref, rwkv_ref, gla_ref, gate_ref, merge_ref):
    h = _rms(x_ref[...], g_ref[...]).astype(BF16)
    off = 0
    for ref, n in zip((mla_ref, rwkv_ref, gla_ref, gate_ref, merge_ref), _PROJ_SPLITS):
        ref[...] = _dot(h, w_ref[:, off:off + n]).astype(ref.dtype)
        off += n


def _inproj(x2, g, w, tm):
    N = x2.shape[0]
    return pl.pallas_call(
        _inproj_kernel,
        grid=(N // tm,),
        in_specs=[pl.BlockSpec((tm, D_MODEL), lambda i: (i, 0)),
                  pl.BlockSpec((1, D_MODEL), lambda i: (0, 0)),
                  pl.BlockSpec((D_MODEL, PROJ_W), lambda i: (0, 0), pipeline_mode=pl.Buffered(1))],
        out_specs=[pl.BlockSpec((tm, n), lambda i: (i, 0)) for n in _PROJ_SPLITS],
        out_shape=[jax.ShapeDtypeStruct((N, n), F32) for n in _PROJ_SPLITS],
        compiler_params=pltpu.CompilerParams(dimension_semantics=("parallel",), vmem_limit_bytes=VMEM_LIMIT),
        name="inproj",
    )(x2, g, w)


def _prep_w_in(w_in):
    offs = [0]
    for n in IN_SIZES:
        offs.append(offs[-1] + n)
    c_q, c_kv, k_rope, u_rwkv, g_q, g_k, g_v, g_lat, br_gate, merge_gate = (
        w_in[:, offs[i]:offs[i + 1]] for i in range(len(IN_SIZES)))
    z = lambda n: jnp.zeros((D_MODEL, n), w_in.dtype)
    half = MLA_ROPE // 2
    k_rot = jnp.concatenate([-k_rope[:, half:], k_rope[:, :half]], axis=1)
    cols = [c_q, c_kv,
            z(MLA_NOPE), k_rope, z(QK_W - MLA_NOPE - MLA_ROPE),
            z(MLA_NOPE), k_rot, z(QK_W - MLA_NOPE - MLA_ROPE),
            u_rwkv,
            g_q, g_k, g_v, g_lat, z(GLA_IN_W - 1024 - GLA_GATE_RANK),
            br_gate, merge_gate]
    return jnp.concatenate(cols, axis=1).astype(BF16)


def _mla_prep_kernel(p_ref, cos_ref, sin_ref, qn_ref, kvn_ref, wq_ref, wqr_ref, wk_ref, wv_ref,
                     q_ref, k_ref, v_ref):
    cm = cos_ref[...]
    sm = sin_ref[...]
    scale = (MLA_NOPE + MLA_ROPE) ** -0.5
    cq = _rms(p_ref[:, 0:MLA_Q_LORA], qn_ref[...]).astype(BF16)
    q_all = _dot(cq, wq_ref[...])
    q_rot = _dot(cq, wqr_ref[...])
    ckv = _rms(p_ref[:, MLA_Q_LORA:MLA_Q_LORA + MLA_KV_LORA], kvn_ref[...]).astype(BF16)
    k_all = _dot(ckv, wk_ref[...])
    v_all = _dot(ckv, wv_ref[...])
    o = MLA_Q_LORA + MLA_KV_LORA
    k_r = p_ref[:, o:o + QK_W] * cm + p_ref[:, o + QK_W:o + 2 * QK_W] * sm
    for h in range(MLA_HEADS):
        sl = slice(h * QK_W, (h + 1) * QK_W)
        q_ref[h] = ((q_all[:, sl] * cm + q_rot[:, sl] * sm) * scale).astype(q_ref.dtype)
        k_ref[h] = (k_all[:, sl] + k_r).astype(k_ref.dtype)
    for hp in range(MLA_HEADS // 2):
        v_ref[hp] = v_all[:, hp * LANE:(hp + 1) * LANE].astype(v_ref.dtype)


def _prep_mla_weights(w_uq, w_ukv):
    H = MLA_HEADS
    half = MLA_ROPE // 2
    wq = w_uq.reshape(MLA_Q_LORA, H, MLA_NOPE + MLA_ROPE)
    nope, rope = wq[..., :MLA_NOPE], wq[..., MLA_NOPE:]
    rot = jnp.concatenate([-rope[..., half:], rope[..., :half]], axis=-1)
    zq = lambda n: jnp.zeros((MLA_Q_LORA, H, n), w_uq.dtype)
    pad = QK_W - MLA_NOPE - MLA_ROPE
    wq_main = jnp.concatenate([nope, rope, zq(pad)], axis=-1).reshape(MLA_Q_LORA, H * QK_W)
    wq_rot = jnp.concatenate([zq(MLA_NOPE), rot, zq(pad)], axis=-1).reshape(MLA_Q_LORA, H * QK_W)
    wkv = w_ukv.reshape(MLA_KV_LORA, H, MLA_NOPE + MLA_V)
    wk = jnp.concatenate([wkv[..., :MLA_NOPE], jnp.zeros((MLA_KV_LORA, H, QK_W - MLA_NOPE), w_ukv.dtype)],
                         axis=-1).reshape(MLA_KV_LORA, H * QK_W)
    wv = wkv[..., MLA_NOPE:].reshape(MLA_KV_LORA, H * MLA_V)
    return wq_main.astype(BF16), wq_rot.astype(BF16), wk.astype(BF16), wv.astype(BF16)


def _mla_prep(p_mla, cosm, sinm, q_norm, kv_norm, wq, wqr, wk, wv, ts):
    B, S, _ = p_mla.shape
    H = MLA_HEADS
    const = lambda shape: pl.BlockSpec(shape, lambda b, i: (0,) * len(shape))
    return pl.pallas_call(
        _mla_prep_kernel,
        grid=(B, S // ts),
        in_specs=[pl.BlockSpec((None, ts, MLA_IN_W), lambda b, i: (b, i, 0)),
                  pl.BlockSpec((None, ts, QK_W), lambda b, i: (b, i, 0)),
                  pl.BlockSpec((None, ts, QK_W), lambda b, i: (b, i, 0)),
                  const((1, MLA_Q_LORA)), const((1, MLA_KV_LORA)),
                  const(wq.shape), const(wqr.shape), const(wk.shape), const(wv.shape)],
        out_specs=[pl.BlockSpec((None, H, ts, QK_W), lambda b, i: (b, 0, i, 0)),
                   pl.BlockSpec((None, H, ts, QK_W), lambda b, i: (b, 0, i, 0)),
                   pl.BlockSpec((None, H // 2, ts, LANE), lambda b, i: (b, 0, i, 0))],
        out_shape=[jax.ShapeDtypeStruct((B, H, S, QK_W), BF16),
                   jax.ShapeDtypeStruct((B, H, S, QK_W), BF16),
                   jax.ShapeDtypeStruct((B, H // 2, S, LANE), BF16)],
        compiler_params=pltpu.CompilerParams(dimension_semantics=("parallel", "parallel"),
                                             vmem_limit_bytes=VMEM_LIMIT),
        name="mla_prep",
    )(p_mla, cosm, sinm, q_norm, kv_norm, wq, wqr, wk, wv)


def _attn_kernel(q_ref, k_ref, v_ref, o_ref, acc_ref, *, tq):
    qi = pl.program_id(2)
    lane = lax.broadcasted_iota(jnp.int32, (1, LANE), 1)
    first = lane < MLA_V
    acc_ref[...] = jnp.zeros_like(acc_ref)
    q0 = q_ref[0]
    q1 = q_ref[1]

    def step(j, carry, masked):
        m0, l0, m1, l1 = carry
        rows = pl.ds(pl.multiple_of(j * tq, tq), tq)
        v = v_ref[rows, :]
        zero = jnp.zeros_like(v)
        outs = []
        for q, e, m, l in ((q0, 0, m0, l0), (q1, 1, m1, l1)):
            s = _dot_nt(q, k_ref[e, rows, :])
            if masked:
                r = lax.broadcasted_iota(jnp.int32, s.shape, 0)
                c = lax.broadcasted_iota(jnp.int32, s.shape, 1)
                s = jnp.where(c <= r, s, -jnp.inf)
            m_new = jnp.maximum(m, jnp.max(s, axis=-1, keepdims=True))
            p = jnp.exp(s - m_new)
            alpha = jnp.exp(m - m_new)
            l_new = alpha * l + jnp.sum(p, axis=-1, keepdims=True)
            ve = jnp.where(first, v, zero) if e == 0 else jnp.where(first, zero, v)
            outs.append((m_new, l_new, alpha, _dot(p.astype(v.dtype), ve)))
        (m0, l0, a0, pv0), (m1, l1, a1, pv1) = outs
        acc_ref[...] = acc_ref[...] * jnp.where(first, a0, a1) + (pv0 + pv1)
        return m0, l0, m1, l1

    neg = jnp.full((tq, 1), -jnp.inf, F32)
    zer = jnp.zeros((tq, 1), F32)
    carry = lax.fori_loop(0, qi, functools.partial(step, masked=False), (neg, zer, neg, zer))
    _, l0, _, l1 = step(qi, carry, True)
    o_ref[...] = (acc_ref[...] / jnp.where(first, l0, l1)).astype(o_ref.dtype)


def _mla_attn(q, k, v, tq):
    B, H, S, _ = q.shape
    return pl.pallas_call(
        functools.partial(_attn_kernel, tq=tq),
        grid=(B, H // 2, S // tq),
        in_specs=[pl.BlockSpec((None, 2, tq, QK_W), lambda b, h, i: (b, h, i, 0)),
                  pl.BlockSpec((None, 2, S, QK_W), lambda b, h, i: (b, h, 0, 0)),
                  pl.BlockSpec((None, None, S, LANE), lambda b, h, i: (b, h, 0, 0))],
        out_specs=pl.BlockSpec((None, tq, LANE), lambda b, h, i: (b, i, h)),
        out_shape=jax.ShapeDtypeStruct((B, S, BRANCH_W), F32),
        scratch_shapes=[pltpu.VMEM((tq, LANE), F32)],
        compiler_params=pltpu.CompilerParams(dimension_semantics=("parallel", "parallel", "arbitrary"),
                                             vmem_limit_bytes=VMEM_LIMIT),
        name="mla_attn",
    )(q, k, v)


def _tri_inverse(n):
    C = n.shape[0]
    r = lax.broadcasted_iota(jnp.int32, (C, C), 0)
    c = lax.broadcasted_iota(jnp.int32, (C, C), 1)
    p = jnp.where(r == c, 1.0, 0.0) + n
    s = n
    steps = max(C.bit_length() - 2, 0)
    for _ in range(steps):
        s = _dot_hi(s, s)
        p = p + _dot_hi(s, p)
    return p


def _rwkv_kernel(u_ref, mu_ref, w0_ref, wup_ref, a0_ref, aup_ref, kk_ref, ka_ref, rk_ref, lnw_ref, lnb_ref,
                 seg_ref, bd_ref, o_ref, state_ref, last_ref):
    C = RWKV_CHUNK
    W = BRANCH_W
    G = RWKV_QUAD

    @pl.when(pl.program_id(1) == 0)
    def _():
        state_ref[...] = jnp.zeros_like(state_ref)
        last_ref[...] = jnp.zeros_like(last_ref)

    u = u_ref[...]
    row = lax.broadcasted_iota(jnp.int32, (C, 1), 0)
    prev = jnp.where(row == 0, last_ref[...], pltpu.roll(u, 1, axis=0))
    last_ref[...] = u[C - 1:C, :]
    u = u + (prev - u) * mu_ref[...]
    r, k, v = u[:, 0:W], u[:, W:2 * W], u[:, 2 * W:3 * W]
    lwa = u[:, 3 * W:]
    seg = seg_ref[...]

    def segsum(x):
        return _dot_exact_rhs(x, seg, parts=2)

    w = -jax.nn.softplus(-(w0_ref[...] + _dot_hi(jnp.tanh(lwa), wup_ref[...]))) - 0.5
    logd = -jnp.exp(w)
    a = jax.nn.sigmoid(a0_ref[...] + _dot_hi(lwa, aup_ref[...]))
    kk = k * kk_ref[...]
    kk = kk / jnp.maximum(jnp.sqrt(segsum(kk * kk)), 1e-12)
    k = k * (1.0 + (a - 1.0) * ka_ref[...])
    bonus = segsum(r * k * rk_ref[...]) * v

    tr = lax.broadcasted_iota(jnp.int32, (C, C), 0)
    tc = lax.broadcasted_iota(jnp.int32, (C, C), 1)
    incl = tc <= tr
    strict = tc < tr
    b = _dot_exact_lhs(jnp.where(incl, 1.0, 0.0).astype(BF16), logd)
    btot = b[C - 1:C, :]
    enb = jnp.exp(-b)
    rt = r * jnp.exp(b)
    alt = -kk * jnp.exp(b - logd)
    beta = kk * a
    bt = beta * enb
    kt = k * enb
    edec = jnp.exp(btot - b)
    bhat = beta * edec
    khat = k * edec
    gam = jnp.exp(btot)

    lane = lax.broadcasted_iota(jnp.int32, (1, G), 1)
    ys = []
    for q in range(W // G):
        sl = slice(q * G, (q + 1) * G)
        ar = jnp.concatenate([alt[:, sl], rt[:, sl]], axis=0)
        btq = bt[:, sl].astype(BF16)
        ktq = kt[:, sl].astype(BF16)
        vq = v[:, sl]
        ht = state_ref[q]
        pr = _dot_nt(ar.astype(BF16), ht.astype(BF16))
        p0, r0 = pr[:C], pr[C:]
        uq = jnp.zeros((C, G), F32)
        yq = jnp.zeros((C, G), F32)
        vq16 = vq.astype(BF16)
        for h in range(G // RWKV_HEAD):
            lm = (lane >= h * RWKV_HEAD) & (lane < (h + 1) * RWKV_HEAD)
            arh = jnp.where(lm, ar, 0.0).astype(BF16)
            ab = _dot_nt(arh, btq)
            ak = _dot_nt(arh, ktq)
            t = _tri_inverse(jnp.where(strict, ab[:C], 0.0))
            a_ak = jnp.where(strict, ak[:C], 0.0)
            a_rb = jnp.where(incl, ab[C:], 0.0)
            a_rk = jnp.where(incl, ak[C:], 0.0)
            rhs = p0 + _dot(a_ak.astype(BF16), vq16)
            uh = _dot_hi(t, rhs)
            yh = r0 + _dot(a_rb.astype(BF16), uh.astype(BF16)) + _dot(a_rk.astype(BF16), vq16)
            uq = jnp.where(lm, uh, uq)
            yq = jnp.where(lm, yh, yq)
        z = jnp.concatenate([vq, uq], axis=0).astype(BF16)
        wk = jnp.concatenate([khat[:, sl], bhat[:, sl]], axis=0).astype(BF16)
        state_ref[q] = ht * gam[:, sl] + bd_ref[...] * _dot_tn(z, wk)
        ys.append(yq)
    y = jnp.concatenate(ys, axis=1)
    mean = segsum(y) * (1.0 / RWKV_HEAD)
    d = y - mean
    var = segsum(d * d) * (1.0 / RWKV_HEAD)
    y = d * lax.rsqrt(var + RWKV_GN_EPS)
    o_ref[...] = (y * lnw_ref[...] + lnb_ref[...] + bonus).astype(o_ref.dtype)


def _block_diag_ones(n, blk_r, blk_c=None, m=None, dtype=F32):
    blk_c = blk_r if blk_c is None else blk_c
    m = n if m is None else m
    r = jnp.arange(n)[:, None] // blk_r
    c = jnp.arange(m)[None, :] // blk_c
    return (r == c).astype(dtype)


def _rwkv(p_rwkv, mu, w0, w_up, a0, a_up, k_k, k_a, r_k, ln_w, ln_b):
    B, S, _ = p_rwkv.shape
    C, W, G = RWKV_CHUNK, BRANCH_W, RWKV_QUAD
    wup = jnp.concatenate([w_up, jnp.zeros_like(a_up)], axis=0)
    aup = jnp.concatenate([jnp.zeros_like(w_up), a_up], axis=0)
    seg = _block_diag_ones(W, RWKV_HEAD, dtype=BF16)
    bd = _block_diag_ones(G, RWKV_HEAD)
    row = lambda n: pl.BlockSpec((1, n), lambda b, c: (0, 0))
    full = lambda a: pl.BlockSpec(a.shape, lambda b, c: (0,) * a.ndim)
    return pl.pallas_call(
        _rwkv_kernel,
        grid=(B, S // C),
        in_specs=[pl.BlockSpec((None, C, RWKV_IN), lambda b, c: (b, c, 0)),
                  row(RWKV_IN), row(W), full(wup), row(W), full(aup), row(W), row(W), row(W), row(W), row(W),
                  full(seg), full(bd)],
        out_specs=pl.BlockSpec((None, C, W), lambda b, c: (b, c, 0)),
        out_shape=jax.ShapeDtypeStruct((B, S, W), F32),
        scratch_shapes=[pltpu.VMEM((W // G, G, G), F32), pltpu.VMEM((1, RWKV_IN), F32)],
        compiler_params=pltpu.CompilerParams(dimension_semantics=("parallel", "arbitrary"),
                                             vmem_limit_bytes=VMEM_LIMIT),
        name="rwkv7",
    )(p_rwkv, mu, w0, wup, a0, aup, k_k, k_a, r_k.reshape(1, W), ln_w, ln_b, seg, bd)


def _gla_kernel(p_ref, aup_ref, ab_ref, gn_ref, seg_ref, bd_ref, o_ref, state_ref, *, nblk):
    c = GLA_BLOCK
    KW = GLA_HEADS * GLA_DK
    VW = GLA_HEADS * GLA_DV

    @pl.when(pl.program_id(1) == 0)
    def _():
        state_ref[...] = jnp.zeros_like(state_ref)

    tr = lax.broadcasted_iota(jnp.int32, (c, c), 0)
    tc = lax.broadcasted_iota(jnp.int32, (c, c), 1)
    tri = jnp.where(tc <= tr, 1.0, 0.0).astype(BF16)
    row = lax.broadcasted_iota(jnp.int32, (c, 1), 0)

    def block(i, carry):
        rows = pl.ds(pl.multiple_of(i * c, c), c)
        q = p_ref[rows, 0:KW] * (GLA_DK ** -0.5)
        k = p_ref[rows, KW:2 * KW]
        v = p_ref[rows, 2 * KW:2 * KW + VW]
        lat = p_ref[rows, 2 * KW + VW:]
        z = _dot_hi(lat, aup_ref[...]) + ab_ref[...]
        log_a = jax.nn.log_sigmoid(z) * (1.0 / GLA_TAU)
        b = _dot_exact_lhs(tri, log_a)
        ht = state_ref[...]
        o = _dot_nt((q * jnp.exp(b)).astype(BF16), ht.astype(BF16))
        xs = []
        for j in range(c):
            diff = jnp.where(row >= j, b - b[j:j + 1, :], -jnp.inf)
            xs.append(q * jnp.exp(diff) * k[j:j + 1, :])
        att = _dot(jnp.concatenate(xs, axis=0).astype(BF16), seg_ref[...])
        for j in range(c):
            o = o + att[j * c:(j + 1) * c, :] * v[j:j + 1, :]
        blast = b[c - 1:c, :]
        khat = (k * jnp.exp(blast - b)).astype(BF16)
        state_ref[...] = ht * jnp.exp(blast) + bd_ref[...] * _dot_tn(v.astype(BF16), khat)
        outs = []
        for h in range(GLA_HEADS):
            oh = o[:, h * GLA_DV:(h + 1) * GLA_DV]
            outs.append(_rms(oh, gn_ref[:, h * GLA_DV:(h + 1) * GLA_DV]))
        o_ref[rows, :] = jnp.concatenate(outs, axis=1).astype(o_ref.dtype)
        return carry

    lax.fori_loop(0, nblk, block, 0)


def _gla(p_gla, a_up, a_b, g_norm, tg):
    B, S, _ = p_gla.shape
    KW = GLA_HEADS * GLA_DK
    VW = GLA_HEADS * GLA_DV
    aup = jnp.concatenate([a_up, jnp.zeros((GLA_IN_W - 2 * KW - VW - GLA_GATE_RANK, KW), a_up.dtype)], axis=0)
    seg = _block_diag_ones(KW, GLA_DK, GLA_DV, VW, dtype=BF16)
    bd = _block_diag_ones(VW, GLA_DV, GLA_DK, KW)
    full = lambda a: pl.BlockSpec(a.shape, lambda b, t: (0,) * a.ndim)
    return pl.pallas_call(
        functools.partial(_gla_kernel, nblk=tg // GLA_BLOCK),
        grid=(B, S // tg),
        in_specs=[pl.BlockSpec((None, tg, GLA_IN_W), lambda b, t: (b, t, 0)),
                  full(aup), pl.BlockSpec((1, KW), lambda b, t: (0, 0)), pl.BlockSpec((1, VW), lambda b, t: (0, 0)),
                  full(seg), full(bd)],
        out_specs=pl.BlockSpec((None, tg, VW), lambda b, t: (b, t, 0)),
        out_shape=jax.ShapeDtypeStruct((B, S, VW), F32),
        scratch_shapes=[pltpu.VMEM((VW, KW), F32)],
        compiler_params=pltpu.CompilerParams(dimension_semantics=("parallel", "arbitrary"),
                                             vmem_limit_bytes=VMEM_LIMIT),
        name="gla",
    )(p_gla, aup, a_b, g_norm, seg, bd)


def _merge_kernel(x_ref, ym_ref, yr_ref, yg_ref, gate_ref, mg_ref, wb_ref, wo_ref, gpost_ref, o_ref):
    W = BRANCH_W
    merged = None
    for n, y_ref in enumerate((ym_ref, yr_ref, yg_ref)):
        ys = (y_ref[...] * jax.nn.silu(gate_ref[:, n * W:(n + 1) * W])).astype(BF16)
        br = _dot(ys, wb_ref[n]) * jax.nn.sigmoid(mg_ref[:, n * D_MODEL:(n + 1) * D_MODEL])
        merged = br if merged is None else merged + br
    out = _dot(merged.astype(BF16), wo_ref[...])
    o_ref[...] = x_ref[...] + _rms(out, gpost_ref[...])


def _merge(x2, ym, yr, yg, gate, mg, wb, wo, gpost, tm):
    N = x2.shape[0]
    W = BRANCH_W
    tile = lambda n: pl.BlockSpec((tm, n), lambda i: (i, 0))
    return pl.pallas_call(
        _merge_kernel,
        grid=(N // tm,),
        in_specs=[tile(D_MODEL), tile(W), tile(W), tile(W), tile(GATE_W), tile(MERGE_W),
                  pl.BlockSpec((N_BRANCH, W, D_MODEL), lambda i: (0, 0, 0)),
                  pl.BlockSpec((D_MODEL, D_MODEL), lambda i: (0, 0)),
                  pl.BlockSpec((1, D_MODEL), lambda i: (0, 0))],
        out_specs=tile(D_MODEL),
        out_shape=jax.ShapeDtypeStruct((N, D_MODEL), F32),
        compiler_params=pltpu.CompilerParams(dimension_semantics=("parallel",), vmem_limit_bytes=VMEM_LIMIT),
        name="merge_out",
    )(x2, ym, yr, yg, gate, mg, wb, wo, gpost)


def _tile(n, want):
    t = min(n, want)
    assert n % t == 0, (n, t)
    return t


def kernel(x, positions, norm_pre, w_in, mla_q_norm, mla_kv_norm, mla_w_uq, mla_w_ukv, rwkv_mu, rwkv_w0,
           rwkv_w_up, rwkv_a0, rwkv_a_up, rwkv_k_k, rwkv_k_a, rwkv_r_k, rwkv_ln_w, rwkv_ln_b, gla_a_up, gla_a_b,
           gla_norm, w_branch_out, w_out, norm_post):
    B, S, D = x.shape
    assert D == D_MODEL and S % RWKV_CHUNK == 0
    depth = w_in.shape[0]
    N = B * S
    ts = _tile(S, 512)
    cosm, sinm = _rope_tables(positions, ts)
    x2 = x.reshape(N, D)
    row = lambda a: a.reshape(1, -1)
    for l in range(depth):
        p_mla, p_rwkv, p_gla, p_gate, p_merge = _inproj(x2, row(norm_pre[l]), _prep_w_in(w_in[l]), _tile(N, 256))
        wq, wqr, wk, wv = _prep_mla_weights(mla_w_uq[l], mla_w_ukv[l])
        q, k, v = _mla_prep(p_mla.reshape(B, S, -1), cosm, sinm, row(mla_q_norm[l]), row(mla_kv_norm[l]),
                            wq, wqr, wk, wv, ts)
        y_mla = _mla_attn(q, k, v, _tile(S, 256))
        y_rwkv = _rwkv(p_rwkv.reshape(B, S, -1), row(rwkv_mu[l]), row(rwkv_w0[l]), rwkv_w_up[l], row(rwkv_a0[l]),
                       rwkv_a_up[l], row(rwkv_k_k[l]), row(rwkv_k_a[l]), rwkv_r_k[l], row(rwkv_ln_w[l]),
                       row(rwkv_ln_b[l]))
        y_gla = _gla(p_gla.reshape(B, S, -1), gla_a_up[l], row(gla_a_b[l]), row(gla_norm[l]), _tile(S, 128))
        x2 = _merge(x2, y_mla.reshape(N, -1), y_rwkv.reshape(N, -1), y_gla.reshape(N, -1), p_gate, p_merge,
                    w_branch_out[l].astype(BF16), w_out[l].astype(BF16), row(norm_post[l]), _tile(N, 256))
    return x2.reshape(B, S, D)
```

```python
import functools

import jax
import jax.numpy as jnp
from jax import lax
from jax.experimental import pallas as pl
from jax.experimental.pallas import tpu as pltpu

F32 = jnp.float32
BF16 = jnp.bfloat16

D_MODEL = 1024
N_BRANCH = 3
BRANCH_W = D_MODEL // 2
NORM_EPS = 1e-6
MLA_HEADS = 8
MLA_NOPE = 64
MLA_ROPE = 32
MLA_V = BRANCH_W // MLA_HEADS
MLA_Q_LORA = 256
MLA_KV_LORA = 128
ROPE_BASE = 10000.0
RWKV_HEADS = 8
RWKV_HEAD = BRANCH_W // RWKV_HEADS
RWKV_DECAY_RANK = 64
RWKV_ICLR_RANK = 64
RWKV_IN = 3 * BRANCH_W + RWKV_DECAY_RANK + RWKV_ICLR_RANK
RWKV_GN_EPS = 64e-5
GLA_HEADS = 4
GLA_DK = 64
GLA_DV = BRANCH_W // GLA_HEADS
GLA_GATE_RANK = 16
GLA_TAU = 16.0
IN_SIZES = (MLA_Q_LORA, MLA_KV_LORA, MLA_ROPE, RWKV_IN,
            GLA_HEADS * GLA_DK, GLA_HEADS * GLA_DK, GLA_HEADS * GLA_DV, GLA_GATE_RANK,
            N_BRANCH * BRANCH_W, N_BRANCH * D_MODEL)

LANE = 128
MLA_IN_W = 640
GLA_IN_W = 1152
GATE_W = N_BRANCH * BRANCH_W
MERGE_W = N_BRANCH * D_MODEL
PROJ_W = MLA_IN_W + RWKV_IN + GLA_IN_W + GATE_W + MERGE_W
QK_W = 128

RWKV_CHUNK = 64
RWKV_QUAD = 256
RWKV_SEQS = 2
GLA_CHUNK = 64
GLA_BLOCK = 16
VMEM_LIMIT = 48 * 1024 * 1024

TILE_ROWS = dict(rope=512, inproj=256, mla_prep=512, attn=512, gla=256, merge=256)


def _tile(n, want):
    t = min(n, want)
    assert n % t == 0, (n, t)
    return t


def _dot(a, b):
    return jnp.dot(a, b, preferred_element_type=F32)


def _dot_nt(a, b):
    return lax.dot_general(a, b, (((1,), (1,)), ((), ())), preferred_element_type=F32)


def _dot_tn(a, b):
    return lax.dot_general(a, b, (((0,), (0,)), ((), ())), preferred_element_type=F32)


def _split(x, parts):
    out = []
    rem = x
    for _ in range(parts):
        t = rem.astype(BF16)
        out.append(t)
        rem = rem - t.astype(F32)
    return out


def _dot_exact_rhs(a, b_bf16, parts=3):
    acc = None
    for t in _split(a, parts):
        d = _dot(t, b_bf16)
        acc = d if acc is None else acc + d
    return acc


def _dot_exact_lhs(a_bf16, b, parts=3):
    acc = None
    for t in _split(b, parts):
        d = _dot(a_bf16, t)
        acc = d if acc is None else acc + d
    return acc


def _dot_hi(a, b):
    a1, a2 = _split(a, 2)
    b1, b2 = _split(b, 2)
    return _dot(a1, b1) + (_dot(a1, b2) + _dot(a2, b1))


def _rms(x, g):
    return x * lax.rsqrt(jnp.mean(x * x, axis=-1, keepdims=True) + NORM_EPS) * g


def _block_diag_ones(n, blk_r, blk_c=None, m=None, dtype=F32):
    blk_c = blk_r if blk_c is None else blk_c
    m = n if m is None else m
    r = jnp.arange(n)[:, None] // blk_r
    c = jnp.arange(m)[None, :] // blk_c
    return (r == c).astype(dtype)


def _rope_kernel(pos_ref, inv_ref, cos_ref, sin_ref):
    ang = pos_ref[...] * inv_ref[...]
    lane = lax.broadcasted_iota(jnp.int32, ang.shape, 1)
    rope = (lane >= MLA_NOPE) & (lane < MLA_NOPE + MLA_ROPE)
    cos_ref[...] = jnp.where(rope, jnp.cos(ang), jnp.where(lane < MLA_NOPE, 1.0, 0.0))
    sin_ref[...] = jnp.where(rope, jnp.sin(ang), 0.0)


def _rope_tables(positions):
    B, S = positions.shape
    ts = _tile(S, TILE_ROWS["rope"])
    inv = 1.0 / (ROPE_BASE ** (jnp.arange(0, MLA_ROPE, 2, dtype=F32) / MLA_ROPE))
    inv_row = jnp.zeros((1, QK_W), F32).at[0, MLA_NOPE:MLA_NOPE + MLA_ROPE].set(jnp.tile(inv, 2))
    pos = positions.astype(F32)[..., None]
    return pl.pallas_call(
        _rope_kernel,
        grid=(B, S // ts),
        in_specs=[pl.BlockSpec((None, ts, 1), lambda b, i: (b, i, 0)),
                  pl.BlockSpec((1, QK_W), lambda b, i: (0, 0))],
        out_specs=[pl.BlockSpec((None, ts, QK_W), lambda b, i: (b, i, 0))] * 2,
        out_shape=[jax.ShapeDtypeStruct((B, S, QK_W), F32)] * 2,
        name="rope_tables",
    )(pos, inv_row)


_PROJ_SPLITS = (MLA_IN_W, RWKV_IN, GLA_IN_W, GATE_W, MERGE_W)


def _inproj_kernel(x_ref, g_ref, w_ref, mla_ref, rwkv_ref, gla_ref, gate_ref, merge_ref):
    h = _rms(x_ref[...], g_ref[...]).astype(BF16)
    off = 0
    for ref, n in zip((mla_ref, rwkv_ref, gla_ref, gate_ref, merge_ref), _PROJ_SPLITS):
        ref[...] = _dot(h, w_ref[:, off:off + n]).astype(ref.dtype)
        off += n


def _inproj(x2, g, w):
    N = x2.shape[0]
    tm = _tile(N, TILE_ROWS["inproj"])
    return pl.pallas_call(
        _inproj_kernel,
        grid=(N // tm,),
        in_specs=[pl.BlockSpec((tm, D_MODEL), lambda i: (i, 0)),
                  pl.BlockSpec((1, D_MODEL), lambda i: (0, 0)),
                  pl.BlockSpec((D_MODEL, PROJ_W), lambda i: (0, 0), pipeline_mode=pl.Buffered(1))],
        out_specs=[pl.BlockSpec((tm, n), lambda i: (i, 0)) for n in _PROJ_SPLITS],
        out_shape=[jax.ShapeDtypeStruct((N, n), F32) for n in _PROJ_SPLITS],
        compiler_params=pltpu.CompilerParams(dimension_semantics=("parallel",), vmem_limit_bytes=VMEM_LIMIT),
        name="inproj",
    )(x2, g, w)


def _prep_w_in(w_in):
    offs = [0]
    for n in IN_SIZES:
        offs.append(offs[-1] + n)
    c_q, c_kv, k_rope, u_rwkv, g_q, g_k, g_v, g_lat, br_gate, merge_gate = (
        w_in[:, offs[i]:offs[i + 1]] for i in range(len(IN_SIZES)))
    z = lambda n: jnp.zeros((D_MODEL, n), w_in.dtype)
    half = MLA_ROPE // 2
    k_rot = jnp.concatenate([-k_rope[:, half:], k_rope[:, :half]], axis=1)
    cols = [c_q, c_kv,
            z(MLA_NOPE), k_rope, z(QK_W - MLA_NOPE - MLA_ROPE),
            z(MLA_NOPE), k_rot, z(QK_W - MLA_NOPE - MLA_ROPE),
            u_rwkv,
            g_q, g_k, g_v, g_lat, z(GLA_IN_W - 1024 - GLA_GATE_RANK),
            br_gate, merge_gate]
    return jnp.concatenate(cols, axis=1).astype(BF16)


def _mla_prep_kernel(p_ref, cos_ref, sin_ref, qn_ref, kvn_ref, wq_ref, wqr_ref, wk_ref, wv_ref,
                     q_ref, k_ref, v_ref):
    cm = cos_ref[...]
    sm = sin_ref[...]
    scale = (MLA_NOPE + MLA_ROPE) ** -0.5
    cq = _rms(p_ref[:, 0:MLA_Q_LORA], qn_ref[...]).astype(BF16)
    q_all = _dot(cq, wq_ref[...])
    q_rot = _dot(cq, wqr_ref[...])
    ckv = _rms(p_ref[:, MLA_Q_LORA:MLA_Q_LORA + MLA_KV_LORA], kvn_ref[...]).astype(BF16)
    k_all = _dot(ckv, wk_ref[...])
    v_all = _dot(ckv, wv_ref[...])
    o = MLA_Q_LORA + MLA_KV_LORA
    k_r = p_ref[:, o:o + QK_W] * cm + p_ref[:, o + QK_W:o + 2 * QK_W] * sm
    for h in range(MLA_HEADS):
        sl = slice(h * QK_W, (h + 1) * QK_W)
        q_ref[h] = ((q_all[:, sl] * cm + q_rot[:, sl] * sm) * scale).astype(q_ref.dtype)
        k_ref[h] = (k_all[:, sl] + k_r).astype(k_ref.dtype)
    for hp in range(MLA_HEADS // 2):
        v_ref[hp] = v_all[:, hp * LANE:(hp + 1) * LANE].astype(v_ref.dtype)


def _prep_mla_weights(w_uq, w_ukv):
    H = MLA_HEADS
    half = MLA_ROPE // 2
    wq = w_uq.reshape(MLA_Q_LORA, H, MLA_NOPE + MLA_ROPE)
    nope, rope = wq[..., :MLA_NOPE], wq[..., MLA_NOPE:]
    rot = jnp.concatenate([-rope[..., half:], rope[..., :half]], axis=-1)
    zq = lambda n: jnp.zeros((MLA_Q_LORA, H, n), w_uq.dtype)
    pad = QK_W - MLA_NOPE - MLA_ROPE
    wq_main = jnp.concatenate([nope, rope, zq(pad)], axis=-1).reshape(MLA_Q_LORA, H * QK_W)
    wq_rot = jnp.concatenate([zq(MLA_NOPE), rot, zq(pad)], axis=-1).reshape(MLA_Q_LORA, H * QK_W)
    wkv = w_ukv.reshape(MLA_KV_LORA, H, MLA_NOPE + MLA_V)
    wk = jnp.concatenate([wkv[..., :MLA_NOPE], jnp.zeros((MLA_KV_LORA, H, QK_W - MLA_NOPE), w_ukv.dtype)],
                         axis=-1).reshape(MLA_KV_LORA, H * QK_W)
    wv = wkv[..., MLA_NOPE:].reshape(MLA_KV_LORA, H * MLA_V)
    return wq_main.astype(BF16), wq_rot.astype(BF16), wk.astype(BF16), wv.astype(BF16)


def _mla_prep(p_mla, cosm, sinm, q_norm, kv_norm, wq, wqr, wk, wv):
    B, S, _ = p_mla.shape
    H = MLA_HEADS
    ts = _tile(S, TILE_ROWS["mla_prep"])
    const = lambda shape: pl.BlockSpec(shape, lambda b, i: (0,) * len(shape))
    return pl.pallas_call(
        _mla_prep_kernel,
        grid=(B, S // ts),
        in_specs=[pl.BlockSpec((None, ts, MLA_IN_W), lambda b, i: (b, i, 0)),
                  pl.BlockSpec((None, ts, QK_W), lambda b, i: (b, i, 0)),
                  pl.BlockSpec((None, ts, QK_W), lambda b, i: (b, i, 0)),
                  const((1, MLA_Q_LORA)), const((1, MLA_KV_LORA)),
                  const(wq.shape), const(wqr.shape), const(wk.shape), const(wv.shape)],
        out_specs=[pl.BlockSpec((None, H, ts, QK_W), lambda b, i: (b, 0, i, 0)),
                   pl.BlockSpec((None, H, ts, QK_W), lambda b, i: (b, 0, i, 0)),
                   pl.BlockSpec((None, H // 2, ts, LANE), lambda b, i: (b, 0, i, 0))],
        out_shape=[jax.ShapeDtypeStruct((B, H, S, QK_W), BF16),
                   jax.ShapeDtypeStruct((B, H, S, QK_W), BF16),
                   jax.ShapeDtypeStruct((B, H // 2, S, LANE), BF16)],
        compiler_params=pltpu.CompilerParams(dimension_semantics=("parallel", "parallel"),
                                             vmem_limit_bytes=VMEM_LIMIT),
        name="mla_prep",
    )(p_mla, cosm, sinm, q_norm, kv_norm, wq, wqr, wk, wv)


def _attn_kernel(q_ref, k_ref, v_ref, o_ref, acc_ref, *, tq):
    qi = pl.program_id(2)
    lane = lax.broadcasted_iota(jnp.int32, (1, LANE), 1)
    first = lane < MLA_V
    acc_ref[...] = jnp.zeros_like(acc_ref)
    q0 = q_ref[0]
    q1 = q_ref[1]

    def step(j, carry, masked):
        m0, l0, m1, l1 = carry
        rows = pl.ds(pl.multiple_of(j * tq, tq), tq)
        v = v_ref[rows, :]
        zero = jnp.zeros_like(v)
        outs = []
        for q, e, m, l in ((q0, 0, m0, l0), (q1, 1, m1, l1)):
            s = _dot_nt(q, k_ref[e, rows, :])
            if masked:
                r = lax.broadcasted_iota(jnp.int32, s.shape, 0)
                c = lax.broadcasted_iota(jnp.int32, s.shape, 1)
                s = jnp.where(c <= r, s, -jnp.inf)
            m_new = jnp.maximum(m, jnp.max(s, axis=-1, keepdims=True))
            p = jnp.exp(s - m_new)
            alpha = jnp.exp(m - m_new)
            l_new = alpha * l + jnp.sum(p, axis=-1, keepdims=True)
            ve = jnp.where(first, v, zero) if e == 0 else jnp.where(first, zero, v)
            outs.append((m_new, l_new, alpha, _dot(p.astype(v.dtype), ve)))
        (m0, l0, a0, pv0), (m1, l1, a1, pv1) = outs
        acc_ref[...] = acc_ref[...] * jnp.where(first, a0, a1) + (pv0 + pv1)
        return m0, l0, m1, l1

    neg = jnp.full((tq, 1), -jnp.inf, F32)
    zer = jnp.zeros((tq, 1), F32)
    carry = lax.fori_loop(0, qi, functools.partial(step, masked=False), (neg, zer, neg, zer))
    _, l0, _, l1 = step(qi, carry, True)
    o_ref[...] = (acc_ref[...] / jnp.where(first, l0, l1)).astype(o_ref.dtype)


def _mla_attn(q, k, v):
    B, H, S, _ = q.shape
    tq = _tile(S, TILE_ROWS["attn"])
    return pl.pallas_call(
        functools.partial(_attn_kernel, tq=tq),
        grid=(B, H // 2, S // tq),
        in_specs=[pl.BlockSpec((None, 2, tq, QK_W), lambda b, h, i: (b, h, i, 0)),
                  pl.BlockSpec((None, 2, S, QK_W), lambda b, h, i: (b, h, 0, 0)),
                  pl.BlockSpec((None, None, S, LANE), lambda b, h, i: (b, h, 0, 0))],
        out_specs=pl.BlockSpec((None, tq, LANE), lambda b, h, i: (b, i, h)),
        out_shape=jax.ShapeDtypeStruct((B, S, BRANCH_W), F32),
        scratch_shapes=[pltpu.VMEM((tq, LANE), F32)],
        compiler_params=pltpu.CompilerParams(dimension_semantics=("parallel", "parallel", "arbitrary"),
                                             vmem_limit_bytes=VMEM_LIMIT),
        name="mla_attn",
    )(q, k, v)


def _expand(x, bd16):
    x16 = x.astype(BF16)
    return jnp.concatenate([x16] * (bd16.shape[0] // x.shape[0]), axis=0) * bd16


def _tri_inverse(ns, eye, bd16):
    C = ns[0].shape[0]
    ps = [eye + n for n in ns]
    ss = [_dot(n.astype(BF16), _expand(n, bd16)) for n in ns]
    for _ in range(C.bit_length() - 3):
        sps = [_dot(jnp.concatenate([s, p], axis=0).astype(BF16), _expand(s, bd16)) for s, p in zip(ss, ps)]
        ss = [sp[:C] for sp in sps]
        ps = [p + sp[C:] for p, sp in zip(ps, sps)]
    return [p + _dot(p.astype(BF16), _expand(s, bd16)) for s, p in zip(ss, ps)]


def _rwkv_kernel(u_ref, mu_ref, w0_ref, wup_ref, a0_ref, aup_ref, kk_ref, ka_ref, rk_ref, lnw_ref, lnb_ref,
                 bd_ref, bd16_ref, o_ref, state_ref, last_ref):
    C = RWKV_CHUNK
    W = BRANCH_W
    G = RWKV_QUAD
    NB = u_ref.shape[0]
    R = NB * C
    quads = [slice(q * G, (q + 1) * G) for q in range(W // G)]
    seqs = [slice(e * C, (e + 1) * C) for e in range(NB)]

    @pl.when(pl.program_id(1) == 0)
    def _():
        state_ref[...] = jnp.zeros_like(state_ref)
        last_ref[...] = jnp.zeros_like(last_ref)

    u = u_ref[...].reshape(R, RWKV_IN)
    row = lax.broadcasted_iota(jnp.int32, (R, 1), 0)
    prev = pltpu.roll(u, 1, axis=0)
    for e in range(NB):
        prev = jnp.where(row == e * C, last_ref[e], prev)
        last_ref[e] = u[(e + 1) * C - 1:(e + 1) * C, :]
    u = u + (prev - u) * mu_ref[...]
    r, k, v = u[:, 0:W], u[:, W:2 * W], u[:, 2 * W:3 * W]
    lwa = u[:, 3 * W:]
    bd16 = bd16_ref[...]

    def segsum(x):
        return jnp.concatenate([_dot_exact_rhs(x[:, sl], bd16, parts=2) for sl in quads], axis=1)

    w = -jax.nn.softplus(-(w0_ref[...] + _dot_hi(jnp.tanh(lwa), wup_ref[...]))) - 0.5
    logd = -jnp.exp(w)
    a = jax.nn.sigmoid(a0_ref[...] + _dot_hi(lwa, aup_ref[...]))
    kk = k * kk_ref[...]
    kk = kk / jnp.maximum(jnp.sqrt(segsum(kk * kk)), 1e-12)
    k = k * (1.0 + (a - 1.0) * ka_ref[...])
    bonus = segsum(r * k * rk_ref[...]) * v

    tr = lax.broadcasted_iota(jnp.int32, (R, R), 0)
    tc = lax.broadcasted_iota(jnp.int32, (R, R), 1)
    tri = jnp.where((tc <= tr) & (tc >= (tr // C) * C), 1.0, 0.0).astype(BF16)
    b = _dot_exact_lhs(tri, logd)
    btot = jnp.concatenate([jnp.broadcast_to(b[sq.stop - 1:sq.stop, :], (C, W)) for sq in seqs], axis=0)
    enb = jnp.exp(-b)
    rt = r * jnp.exp(b)
    alt = -kk * jnp.exp(b - logd)
    beta = kk * a
    bt = beta * enb
    kt = k * enb
    edec = jnp.exp(btot - b)
    bhat = beta * edec
    khat = k * edec
    gam = jnp.exp(btot)

    t_idx = lax.broadcasted_iota(jnp.int32, (C, G), 0)
    s_idx = lax.broadcasted_iota(jnp.int32, (C, G), 1) % C
    incl = s_idx <= t_idx
    strict = s_idx < t_idx
    eye = jnp.where(s_idx == t_idx, 1.0, 0.0)

    chains = [(e, q, sq, sl) for e, sq in enumerate(seqs) for q, sl in enumerate(quads)]
    ars = [jnp.concatenate([alt[sq, sl], rt[sq, sl]], axis=0).astype(BF16) for _, _, sq, sl in chains]
    abs_ = [_dot_nt(ar, _expand(bt[sq, sl], bd16)) for ar, (_, _, sq, sl) in zip(ars, chains)]
    aks = [_dot_nt(ar, _expand(kt[sq, sl], bd16)) for ar, (_, _, sq, sl) in zip(ars, chains)]
    t16s = [t.astype(BF16) for t in _tri_inverse([jnp.where(strict, ab[:C], 0.0) for ab in abs_], eye, bd16)]
    m16s = [_dot(t16, _expand(jnp.where(strict, ak[:C], 0.0), bd16)).astype(BF16) for t16, ak in zip(t16s, aks)]
    a_rbs = [jnp.where(incl, ab[C:], 0.0).astype(BF16) for ab in abs_]
    a_rks = [jnp.where(incl, ak[C:], 0.0).astype(BF16) for ak in aks]
    evs = [_expand(v[sq, sl], bd16) for _, _, sq, sl in chains]
    hts = [state_ref[e, q] for e, q, _, _ in chains]
    prs = [_dot_nt(ar, ht.astype(BF16)) for ar, ht in zip(ars, hts)]
    us = [_dot(t16, _expand(pr[:C], bd16)) + _dot(m16, ev) for t16, pr, m16, ev in zip(t16s, prs, m16s, evs)]
    for (e, q, sq, sl), ht, uq in zip(chains, hts, us):
        z = jnp.concatenate([v[sq, sl], uq], axis=0).astype(BF16)
        wk = jnp.concatenate([khat[sq, sl], bhat[sq, sl]], axis=0).astype(BF16)
        state_ref[e, q] = ht * gam[sq.start:sq.start + 1, sl] + bd_ref[...] * _dot_tn(z, wk)
    ys = [pr[C:] + _dot(a_rb, _expand(uq, bd16)) + _dot(a_rk, ev)
          for pr, a_rb, uq, a_rk, ev in zip(prs, a_rbs, us, a_rks, evs)]
    nq = len(quads)
    y = jnp.concatenate([jnp.concatenate(ys[e * nq:(e + 1) * nq], axis=1) for e in range(NB)], axis=0)
    mean = segsum(y) * (1.0 / RWKV_HEAD)
    d = y - mean
    var = segsum(d * d) * (1.0 / RWKV_HEAD)
    y = d * lax.rsqrt(var + RWKV_GN_EPS)
    o_ref[...] = (y * lnw_ref[...] + lnb_ref[...] + bonus).reshape(NB, C, W).astype(o_ref.dtype)


def _rwkv(p_rwkv, mu, w0, w_up, a0, a_up, k_k, k_a, r_k, ln_w, ln_b):
    B, S, _ = p_rwkv.shape
    C, W, G = RWKV_CHUNK, BRANCH_W, RWKV_QUAD
    NB = _tile(B, RWKV_SEQS)
    assert C == RWKV_HEAD
    wup = jnp.concatenate([w_up, jnp.zeros_like(a_up)], axis=0)
    aup = jnp.concatenate([jnp.zeros_like(w_up), a_up], axis=0)
    bd = _block_diag_ones(G, RWKV_HEAD)
    row = lambda n: pl.BlockSpec((1, n), lambda b, c: (0, 0))
    full = lambda a: pl.BlockSpec(a.shape, lambda b, c: (0,) * a.ndim)
    return pl.pallas_call(
        _rwkv_kernel,
        grid=(B // NB, S // C),
        in_specs=[pl.BlockSpec((NB, C, RWKV_IN), lambda b, c: (b, c, 0)),
                  row(RWKV_IN), row(W), full(wup), row(W), full(aup), row(W), row(W), row(W), row(W), row(W),
                  full(bd), full(bd)],
        out_specs=pl.BlockSpec((NB, C, W), lambda b, c: (b, c, 0)),
        out_shape=jax.ShapeDtypeStruct((B, S, W), F32),
        scratch_shapes=[pltpu.VMEM((NB, W // G, G, G), F32), pltpu.VMEM((NB, 1, RWKV_IN), F32)],
        compiler_params=pltpu.CompilerParams(dimension_semantics=("parallel", "arbitrary"),
                                             vmem_limit_bytes=VMEM_LIMIT),
        name="rwkv7",
    )(p_rwkv, mu, w0, wup, a0, aup, k_k, k_a, r_k.reshape(1, W), ln_w, ln_b, bd, bd.astype(BF16))


def _gla_kernel(p_ref, aup_ref, ab_ref, gn_ref, seg_ref, bdk_ref, bd_ref, o_ref, state_ref, *, nchunk):
    C = GLA_CHUNK
    c = GLA_BLOCK
    nb = C // c
    KW = GLA_HEADS * GLA_DK
    VW = GLA_HEADS * GLA_DV
    reps = GLA_HEADS

    @pl.when(pl.program_id(1) == 0)
    def _():
        state_ref[...] = jnp.zeros_like(state_ref)

    tr = lax.broadcasted_iota(jnp.int32, (C, C), 0)
    tc = lax.broadcasted_iota(jnp.int32, (C, C), 1)
    tri = jnp.where(tc <= tr, 1.0, 0.0).astype(BF16)
    row_c = lax.broadcasted_iota(jnp.int32, (c, 1), 0)
    row_C = lax.broadcasted_iota(jnp.int32, (C, 1), 0)

    def chunk(i, carry):
        rows = pl.ds(pl.multiple_of(i * C, C), C)
        q = p_ref[rows, 0:KW] * (GLA_DK ** -0.5)
        k = p_ref[rows, KW:2 * KW]
        v = p_ref[rows, 2 * KW:2 * KW + VW]
        lat = p_ref[rows, 2 * KW + VW:]
        z = _dot_hi(lat, aup_ref[...]) + ab_ref[...]
        log_a = jax.nn.log_sigmoid(z) * (1.0 / GLA_TAU)
        b = _dot_exact_lhs(tri, log_a)
        ht = state_ref[...]
        o_inter = _dot_nt((q * jnp.exp(b)).astype(BF16), ht.astype(BF16))

        v16 = v.astype(BF16)
        ev = jnp.concatenate([v16] * reps, axis=0) * seg_ref[...]
        atts = []
        for blk in range(1, nb):
            lo = blk * c
            m = b[lo - 1:lo, :]
            qb = q[lo:lo + c] * jnp.exp(b[lo:lo + c] - m)
            kb = (k * jnp.exp(jnp.where(row_C < lo, m - b, -jnp.inf))).astype(BF16)
            ke = jnp.concatenate([kb] * reps, axis=0) * bdk_ref[...]
            atts.append(_dot_nt(qb.astype(BF16), ke))
        o_off = _dot(jnp.concatenate(atts, axis=0).astype(BF16), ev)

        outs = []
        for blk in range(nb):
            lo = blk * c
            bb, qq, kk, vv = b[lo:lo + c], q[lo:lo + c], k[lo:lo + c], v[lo:lo + c]
            xs = []
            for j in range(c):
                diff = jnp.where(row_c >= j, bb - bb[j:j + 1, :], -jnp.inf)
                xs.append(qq * jnp.exp(diff) * kk[j:j + 1, :])
            att = _dot(jnp.concatenate(xs, axis=0).astype(BF16), seg_ref[...])
            o = o_inter[lo:lo + c]
            if blk:
                o = o + o_off[lo - c:lo]
            for j in range(c):
                o = o + att[j * c:(j + 1) * c, :] * vv[j:j + 1, :]
            outs.append(o)
        o = jnp.concatenate(outs, axis=0)

        blast = b[C - 1:C, :]
        khat = (k * jnp.exp(blast - b)).astype(BF16)
        state_ref[...] = ht * jnp.exp(blast) + bd_ref[...] * _dot_tn(v16, khat)
        heads = []
        for h in range(GLA_HEADS):
            hs = slice(h * GLA_DV, (h + 1) * GLA_DV)
            heads.append(_rms(o[:, hs], gn_ref[:, hs]))
        o_ref[rows, :] = jnp.concatenate(heads, axis=1).astype(o_ref.dtype)
        return carry

    lax.fori_loop(0, nchunk, chunk, 0)


def _gla(p_gla, a_up, a_b, g_norm):
    B, S, _ = p_gla.shape
    KW = GLA_HEADS * GLA_DK
    VW = GLA_HEADS * GLA_DV
    tg = _tile(S, TILE_ROWS["gla"])
    assert GLA_CHUNK == GLA_DK and tg % GLA_CHUNK == 0
    aup = jnp.concatenate([a_up, jnp.zeros((GLA_IN_W - 2 * KW - VW - GLA_GATE_RANK, KW), a_up.dtype)], axis=0)
    seg = _block_diag_ones(KW, GLA_DK, GLA_DV, VW, dtype=BF16)
    bdk = _block_diag_ones(KW, GLA_DK, dtype=BF16)
    bd = _block_diag_ones(VW, GLA_DV, GLA_DK, KW)
    full = lambda a: pl.BlockSpec(a.shape, lambda b, t: (0,) * a.ndim)
    return pl.pallas_call(
        functools.partial(_gla_kernel, nchunk=tg // GLA_CHUNK),
        grid=(B, S // tg),
        in_specs=[pl.BlockSpec((None, tg, GLA_IN_W), lambda b, t: (b, t, 0)),
                  full(aup), pl.BlockSpec((1, KW), lambda b, t: (0, 0)), pl.BlockSpec((1, VW), lambda b, t: (0, 0)),
                  full(seg), full(bdk), full(bd)],
        out_specs=pl.BlockSpec((None, tg, VW), lambda b, t: (b, t, 0)),
        out_shape=jax.ShapeDtypeStruct((B, S, VW), F32),
        scratch_shapes=[pltpu.VMEM((VW, KW), F32)],
        compiler_params=pltpu.CompilerParams(dimension_semantics=("parallel", "arbitrary"),
                                             vmem_limit_bytes=VMEM_LIMIT),
        name="gla",
    )(p_gla, aup, a_b, g_norm, seg, bdk, bd)


def _merge_kernel(x_ref, ym_ref, yr_ref, yg_ref, gate_ref, mg_ref, wb_ref, wo_ref, gpost_ref, o_ref):
    W = BRANCH_W
    merged = None
    for n, y_ref in enumerate((ym_ref, yr_ref, yg_ref)):
        ys = (y_ref[...] * jax.nn.silu(gate_ref[:, n * W:(n + 1) * W])).astype(BF16)
        br = _dot(ys, wb_ref[n]) * jax.nn.sigmoid(mg_ref[:, n * D_MODEL:(n + 1) * D_MODEL])
        merged = br if merged is None else merged + br
    out = _dot(merged.astype(BF16), wo_ref[...])
    o_ref[...] = x_ref[...] + _rms(out, gpost_ref[...])


def _merge(x2, ym, yr, yg, gate, mg, wb, wo, gpost):
    N = x2.shape[0]
    W = BRANCH_W
    tm = _tile(N, TILE_ROWS["merge"])
    tile = lambda n: pl.BlockSpec((tm, n), lambda i: (i, 0))
    return pl.pallas_call(
        _merge_kernel,
        grid=(N // tm,),
        in_specs=[tile(D_MODEL), tile(W), tile(W), tile(W), tile(GATE_W), tile(MERGE_W),
                  pl.BlockSpec((N_BRANCH, W, D_MODEL), lambda i: (0, 0, 0)),
                  pl.BlockSpec((D_MODEL, D_MODEL), lambda i: (0, 0)),
                  pl.BlockSpec((1, D_MODEL), lambda i: (0, 0))],
        out_specs=tile(D_MODEL),
        out_shape=jax.ShapeDtypeStruct((N, D_MODEL), F32),
        compiler_params=pltpu.CompilerParams(dimension_semantics=("parallel",), vmem_limit_bytes=VMEM_LIMIT),
        name="merge_out",
    )(x2, ym, yr, yg, gate, mg, wb, wo, gpost)


def kernel(x, positions, norm_pre, w_in, mla_q_norm, mla_kv_norm, mla_w_uq, mla_w_ukv, rwkv_mu, rwkv_w0,
           rwkv_w_up, rwkv_a0, rwkv_a_up, rwkv_k_k, rwkv_k_a, rwkv_r_k, rwkv_ln_w, rwkv_ln_b, gla_a_up, gla_a_b,
           gla_norm, w_branch_out, w_out, norm_post):
    B, S, D = x.shape
    assert D == D_MODEL and S % RWKV_CHUNK == 0
    depth = w_in.shape[0]
    N = B * S
    cosm, sinm = _rope_tables(positions)
    x2 = x.reshape(N, D)
    row = lambda a: a.reshape(1, -1)
    for l in range(depth):
        p_mla, p_rwkv, p_gla, p_gate, p_merge = _inproj(x2, row(norm_pre[l]), _prep_w_in(w_in[l]))
        wq, wqr, wk, wv = _prep_mla_weights(mla_w_uq[l], mla_w_ukv[l])
        q, k, v = _mla_prep(p_mla.reshape(B, S, -1), cosm, sinm, row(mla_q_norm[l]), row(mla_kv_norm[l]),
                            wq, wqr, wk, wv)
        y_mla = _mla_attn(q, k, v)
        y_rwkv = _rwkv(p_rwkv.reshape(B, S, -1), row(rwkv_mu[l]), row(rwkv_w0[l]), rwkv_w_up[l], row(rwkv_a0[l]),
                       rwkv_a_up[l], row(rwkv_k_k[l]), row(rwkv_k_a[l]), rwkv_r_k[l], row(rwkv_ln_w[l]),
                       row(rwkv_ln_b[l]))
        y_gla = _gla(p_gla.reshape(B, S, -1), gla_a_up[l], row(gla_a_b[l]), row(gla_norm[l]))
        x2 = _merge(x2, y_mla.reshape(N, -1), y_rwkv.reshape(N, -1), y_gla.reshape(N, -1), p_gate, p_merge,
                    w_branch_out[l].astype(BF16), w_out[l].astype(BF16), row(norm_post[l]))
    return x2.reshape(B, S, D)
```

```python
import functools

import jax
import jax.numpy as jnp
from jax import lax
from jax.experimental import pallas as pl
from jax.experimental.pallas import tpu as pltpu

F32 = jnp.float32
BF16 = jnp.bfloat16

D_MODEL = 1024
N_BRANCH = 3
BRANCH_W = D_MODEL // 2
NORM_EPS = 1e-6
MLA_HEADS = 8
MLA_NOPE = 64
MLA_ROPE = 32
MLA_V = BRANCH_W // MLA_HEADS
MLA_Q_LORA = 256
MLA_KV_LORA = 128
ROPE_BASE = 10000.0
RWKV_HEADS = 8
RWKV_HEAD = BRANCH_W // RWKV_HEADS
RWKV_DECAY_RANK = 64
RWKV_ICLR_RANK = 64
RWKV_IN = 3 * BRANCH_W + RWKV_DECAY_RANK + RWKV_ICLR_RANK
RWKV_GN_EPS = 64e-5
GLA_HEADS = 4
GLA_DK = 64
GLA_DV = BRANCH_W // GLA_HEADS
GLA_GATE_RANK = 16
GLA_TAU = 16.0
IN_SIZES = (MLA_Q_LORA, MLA_KV_LORA, MLA_ROPE, RWKV_IN,
            GLA_HEADS * GLA_DK, GLA_HEADS * GLA_DK, GLA_HEADS * GLA_DV, GLA_GATE_RANK,
            N_BRANCH * BRANCH_W, N_BRANCH * D_MODEL)

LANE = 128
MLA_IN_W = 640
GLA_IN_W = 1152
GATE_W = N_BRANCH * BRANCH_W
MERGE_W = N_BRANCH * D_MODEL
PROJ_W = MLA_IN_W + RWKV_IN + GLA_IN_W + GATE_W + MERGE_W
QK_W = 128

RWKV_CHUNK = 64
RWKV_QUAD = 256
RWKV_SEQS = 4
GLA_CHUNK = 64
GLA_BLOCK = 16
VMEM_LIMIT = 48 * 1024 * 1024

TILE_ROWS = dict(rope=512, inproj=256, mla_prep=512, attn=512, gla=256, merge=256)


def _tile(n, want):
    t = min(n, want)
    assert n % t == 0, (n, t)
    return t


def _dot(a, b):
    return jnp.dot(a, b, preferred_element_type=F32)


def _dot_nt(a, b):
    return lax.dot_general(a, b, (((1,), (1,)), ((), ())), preferred_element_type=F32)


def _dot_tn(a, b):
    return lax.dot_general(a, b, (((0,), (0,)), ((), ())), preferred_element_type=F32)


def _split(x, parts):
    out = []
    rem = x
    for _ in range(parts):
        t = rem.astype(BF16)
        out.append(t)
        rem = rem - t.astype(F32)
    return out


def _dot_exact_rhs(a, b_bf16, parts=3):
    acc = None
    for t in _split(a, parts):
        d = _dot(t, b_bf16)
        acc = d if acc is None else acc + d
    return acc


def _dot_exact_lhs(a_bf16, b, parts=3):
    acc = None
    for t in _split(b, parts):
        d = _dot(a_bf16, t)
        acc = d if acc is None else acc + d
    return acc


def _dot_hi(a, b):
    a1, a2 = _split(a, 2)
    b1, b2 = _split(b, 2)
    return _dot(a1, b1) + (_dot(a1, b2) + _dot(a2, b1))


def _rms(x, g):
    return x * lax.rsqrt(jnp.mean(x * x, axis=-1, keepdims=True) + NORM_EPS) * g


def _block_diag_ones(n, blk_r, blk_c=None, m=None, dtype=F32):
    blk_c = blk_r if blk_c is None else blk_c
    m = n if m is None else m
    r = jnp.arange(n)[:, None] // blk_r
    c = jnp.arange(m)[None, :] // blk_c
    return (r == c).astype(dtype)


def _rope_kernel(pos_ref, inv_ref, cos_ref, sin_ref):
    ang = pos_ref[...] * inv_ref[...]
    lane = lax.broadcasted_iota(jnp.int32, ang.shape, 1)
    rope = (lane >= MLA_NOPE) & (lane < MLA_NOPE + MLA_ROPE)
    cos_ref[...] = jnp.where(rope, jnp.cos(ang), jnp.where(lane < MLA_NOPE, 1.0, 0.0))
    sin_ref[...] = jnp.where(rope, jnp.sin(ang), 0.0)


def _rope_tables(positions):
    B, S = positions.shape
    ts = _tile(S, TILE_ROWS["rope"])
    inv = 1.0 / (ROPE_BASE ** (jnp.arange(0, MLA_ROPE, 2, dtype=F32) / MLA_ROPE))
    inv_row = jnp.zeros((1, QK_W), F32).at[0, MLA_NOPE:MLA_NOPE + MLA_ROPE].set(jnp.tile(inv, 2))
    pos = positions.astype(F32)[..., None]
    return pl.pallas_call(
        _rope_kernel,
        grid=(B, S // ts),
        in_specs=[pl.BlockSpec((None, ts, 1), lambda b, i: (b, i, 0)),
                  pl.BlockSpec((1, QK_W), lambda b, i: (0, 0))],
        out_specs=[pl.BlockSpec((None, ts, QK_W), lambda b, i: (b, i, 0))] * 2,
        out_shape=[jax.ShapeDtypeStruct((B, S, QK_W), F32)] * 2,
        name="rope_tables",
    )(pos, inv_row)


_PROJ_SPLITS = (MLA_IN_W, RWKV_IN, GLA_IN_W, GATE_W, MERGE_W)
_PROJ_DTYPES = (F32, F32, F32, BF16, BF16)
BRANCH_DTYPE = BF16


def _inproj_kernel(x_ref, g_ref, w_ref, mla_ref, rwkv_ref, gla_ref, gate_ref, merge_ref):
    h = _rms(x_ref[...], g_ref[...]).astype(BF16)
    off = 0
    for ref, n in zip((mla_ref, rwkv_ref, gla_ref, gate_ref, merge_ref), _PROJ_SPLITS):
        ref[...] = _dot(h, w_ref[:, off:off + n]).astype(ref.dtype)
        off += n


def _inproj(x2, g, w):
    N = x2.shape[0]
    tm = _tile(N, TILE_ROWS["inproj"])
    return pl.pallas_call(
        _inproj_kernel,
        grid=(N // tm,),
        in_specs=[pl.BlockSpec((tm, D_MODEL), lambda i: (i, 0)),
                  pl.BlockSpec((1, D_MODEL), lambda i: (0, 0)),
                  pl.BlockSpec((D_MODEL, PROJ_W), lambda i: (0, 0), pipeline_mode=pl.Buffered(1))],
        out_specs=[pl.BlockSpec((tm, n), lambda i: (i, 0)) for n in _PROJ_SPLITS],
        out_shape=[jax.ShapeDtypeStruct((N, n), dt) for n, dt in zip(_PROJ_SPLITS, _PROJ_DTYPES)],
        compiler_params=pltpu.CompilerParams(dimension_semantics=("parallel",), vmem_limit_bytes=VMEM_LIMIT),
        name="inproj",
    )(x2, g, w)


def _prep_w_in(w_in):
    offs = [0]
    for n in IN_SIZES:
        offs.append(offs[-1] + n)
    c_q, c_kv, k_rope, u_rwkv, g_q, g_k, g_v, g_lat, br_gate, merge_gate = (
        w_in[:, offs[i]:offs[i + 1]] for i in range(len(IN_SIZES)))
    z = lambda n: jnp.zeros((D_MODEL, n), w_in.dtype)
    half = MLA_ROPE // 2
    k_rot = jnp.concatenate([-k_rope[:, half:], k_rope[:, :half]], axis=1)
    cols = [c_q, c_kv,
            z(MLA_NOPE), k_rope, z(QK_W - MLA_NOPE - MLA_ROPE),
            z(MLA_NOPE), k_rot, z(QK_W - MLA_NOPE - MLA_ROPE),
            u_rwkv,
            g_q, g_k, g_v, g_lat, z(GLA_IN_W - 1024 - GLA_GATE_RANK),
            br_gate, merge_gate]
    return jnp.concatenate(cols, axis=1).astype(BF16)


def _mla_prep_kernel(p_ref, cos_ref, sin_ref, qn_ref, kvn_ref, wq_ref, wqr_ref, wk_ref, wv_ref,
                     q_ref, k_ref, v_ref):
    cm = cos_ref[...]
    sm = sin_ref[...]
    scale = (MLA_NOPE + MLA_ROPE) ** -0.5
    cq = _rms(p_ref[:, 0:MLA_Q_LORA], qn_ref[...]).astype(BF16)
    q_all = _dot(cq, wq_ref[...])
    q_rot = _dot(cq, wqr_ref[...])
    ckv = _rms(p_ref[:, MLA_Q_LORA:MLA_Q_LORA + MLA_KV_LORA], kvn_ref[...]).astype(BF16)
    k_all = _dot(ckv, wk_ref[...])
    v_all = _dot(ckv, wv_ref[...])
    o = MLA_Q_LORA + MLA_KV_LORA
    k_r = p_ref[:, o:o + QK_W] * cm + p_ref[:, o + QK_W:o + 2 * QK_W] * sm
    for h in range(MLA_HEADS):
        sl = slice(h * QK_W, (h + 1) * QK_W)
        q_ref[h] = ((q_all[:, sl] * cm + q_rot[:, sl] * sm) * scale).astype(q_ref.dtype)
        k_ref[h] = (k_all[:, sl] + k_r).astype(k_ref.dtype)
    for hp in range(MLA_HEADS // 2):
        v_ref[hp] = v_all[:, hp * LANE:(hp + 1) * LANE].astype(v_ref.dtype)


def _prep_mla_weights(w_uq, w_ukv):
    H = MLA_HEADS
    half = MLA_ROPE // 2
    wq = w_uq.reshape(MLA_Q_LORA, H, MLA_NOPE + MLA_ROPE)
    nope, rope = wq[..., :MLA_NOPE], wq[..., MLA_NOPE:]
    rot = jnp.concatenate([-rope[..., half:], rope[..., :half]], axis=-1)
    zq = lambda n: jnp.zeros((MLA_Q_LORA, H, n), w_uq.dtype)
    pad = QK_W - MLA_NOPE - MLA_ROPE
    wq_main = jnp.concatenate([nope, rope, zq(pad)], axis=-1).reshape(MLA_Q_LORA, H * QK_W)
    wq_rot = jnp.concatenate([zq(MLA_NOPE), rot, zq(pad)], axis=-1).reshape(MLA_Q_LORA, H * QK_W)
    wkv = w_ukv.reshape(MLA_KV_LORA, H, MLA_NOPE + MLA_V)
    wk = jnp.concatenate([wkv[..., :MLA_NOPE], jnp.zeros((MLA_KV_LORA, H, QK_W - MLA_NOPE), w_ukv.dtype)],
                         axis=-1).reshape(MLA_KV_LORA, H * QK_W)
    wv = wkv[..., MLA_NOPE:].reshape(MLA_KV_LORA, H * MLA_V)
    return wq_main.astype(BF16), wq_rot.astype(BF16), wk.astype(BF16), wv.astype(BF16)


def _mla_prep(p_mla, cosm, sinm, q_norm, kv_norm, wq, wqr, wk, wv):
    B, S, _ = p_mla.shape
    H = MLA_HEADS
    ts = _tile(S, TILE_ROWS["mla_prep"])
    const = lambda shape: pl.BlockSpec(shape, lambda b, i: (0,) * len(shape))
    return pl.pallas_call(
        _mla_prep_kernel,
        grid=(B, S // ts),
        in_specs=[pl.BlockSpec((None, ts, MLA_IN_W), lambda b, i: (b, i, 0)),
                  pl.BlockSpec((None, ts, QK_W), lambda b, i: (b, i, 0)),
                  pl.BlockSpec((None, ts, QK_W), lambda b, i: (b, i, 0)),
                  const((1, MLA_Q_LORA)), const((1, MLA_KV_LORA)),
                  const(wq.shape), const(wqr.shape), const(wk.shape), const(wv.shape)],
        out_specs=[pl.BlockSpec((None, H, ts, QK_W), lambda b, i: (b, 0, i, 0)),
                   pl.BlockSpec((None, H, ts, QK_W), lambda b, i: (b, 0, i, 0)),
                   pl.BlockSpec((None, H // 2, ts, LANE), lambda b, i: (b, 0, i, 0))],
        out_shape=[jax.ShapeDtypeStruct((B, H, S, QK_W), BF16),
                   jax.ShapeDtypeStruct((B, H, S, QK_W), BF16),
                   jax.ShapeDtypeStruct((B, H // 2, S, LANE), BF16)],
        compiler_params=pltpu.CompilerParams(dimension_semantics=("parallel", "parallel"),
                                             vmem_limit_bytes=VMEM_LIMIT),
        name="mla_prep",
    )(p_mla, cosm, sinm, q_norm, kv_norm, wq, wqr, wk, wv)


def _attn_kernel(q_ref, k_ref, v_ref, o_ref, acc_ref, *, tq):
    qi = pl.program_id(2)
    lane = lax.broadcasted_iota(jnp.int32, (1, LANE), 1)
    first = lane < MLA_V
    acc_ref[...] = jnp.zeros_like(acc_ref)
    q0 = q_ref[0]
    q1 = q_ref[1]

    def step(j, carry, masked):
        m0, l0, m1, l1 = carry
        rows = pl.ds(pl.multiple_of(j * tq, tq), tq)
        v = v_ref[rows, :]
        zero = jnp.zeros_like(v)
        outs = []
        for q, e, m, l in ((q0, 0, m0, l0), (q1, 1, m1, l1)):
            s = _dot_nt(q, k_ref[e, rows, :])
            if masked:
                r = lax.broadcasted_iota(jnp.int32, s.shape, 0)
                c = lax.broadcasted_iota(jnp.int32, s.shape, 1)
                s = jnp.where(c <= r, s, -jnp.inf)
            m_new = jnp.maximum(m, jnp.max(s, axis=-1, keepdims=True))
            p = jnp.exp(s - m_new)
            alpha = jnp.exp(m - m_new)
            l_new = alpha * l + jnp.sum(p, axis=-1, keepdims=True)
            ve = jnp.where(first, v, zero) if e == 0 else jnp.where(first, zero, v)
            outs.append((m_new, l_new, alpha, _dot(p.astype(v.dtype), ve)))
        (m0, l0, a0, pv0), (m1, l1, a1, pv1) = outs
        acc_ref[...] = acc_ref[...] * jnp.where(first, a0, a1) + (pv0 + pv1)
        return m0, l0, m1, l1

    neg = jnp.full((tq, 1), -jnp.inf, F32)
    zer = jnp.zeros((tq, 1), F32)
    carry = lax.fori_loop(0, qi, functools.partial(step, masked=False), (neg, zer, neg, zer))
    _, l0, _, l1 = step(qi, carry, True)
    o_ref[...] = (acc_ref[...] / jnp.where(first, l0, l1)).astype(o_ref.dtype)


def _mla_attn(q, k, v):
    B, H, S, _ = q.shape
    tq = _tile(S, TILE_ROWS["attn"])
    return pl.pallas_call(
        functools.partial(_attn_kernel, tq=tq),
        grid=(B, H // 2, S // tq),
        in_specs=[pl.BlockSpec((None, 2, tq, QK_W), lambda b, h, i: (b, h, i, 0)),
                  pl.BlockSpec((None, 2, S, QK_W), lambda b, h, i: (b, h, 0, 0)),
                  pl.BlockSpec((None, None, S, LANE), lambda b, h, i: (b, h, 0, 0))],
        out_specs=pl.BlockSpec((None, tq, LANE), lambda b, h, i: (b, i, h)),
        out_shape=jax.ShapeDtypeStruct((B, S, BRANCH_W), BRANCH_DTYPE),
        scratch_shapes=[pltpu.VMEM((tq, LANE), F32)],
        compiler_params=pltpu.CompilerParams(dimension_semantics=("parallel", "parallel", "arbitrary"),
                                             vmem_limit_bytes=VMEM_LIMIT),
        name="mla_attn",
    )(q, k, v)


def _expand(x, bd16):
    x16 = x.astype(BF16)
    C, G = x.shape
    zero = jnp.zeros((C, LANE), BF16)
    blocks = []
    for h in range(bd16.shape[0] // C):
        lo = (h * RWKV_HEAD // LANE) * LANE
        own = x16[:, lo:lo + LANE] * bd16[h * C:(h + 1) * C, lo:lo + LANE]
        blocks.append(jnp.concatenate([own if l0 == lo else zero for l0 in range(0, G, LANE)], axis=1))
    return jnp.concatenate(blocks, axis=0)


def _tri_inverse(ns, eye, bd16):
    C = ns[0].shape[0]
    ps = [eye + n for n in ns]
    ss = [_dot(n.astype(BF16), _expand(n, bd16)) for n in ns]
    for _ in range(C.bit_length() - 3):
        sps = [_dot(jnp.concatenate([s, p], axis=0).astype(BF16), _expand(s, bd16)) for s, p in zip(ss, ps)]
        ss = [sp[:C] for sp in sps]
        ps = [p + sp[C:] for p, sp in zip(ps, sps)]
    return [p + _dot(p.astype(BF16), _expand(s, bd16)) for s, p in zip(ss, ps)]


def _rwkv_kernel(u_ref, mu_ref, w0_ref, wup_ref, a0_ref, aup_ref, kk_ref, ka_ref, rk_ref, lnw_ref, lnb_ref,
                 bd_ref, bd16_ref, o_ref, state_ref, last_ref):
    C = RWKV_CHUNK
    W = BRANCH_W
    G = RWKV_QUAD
    NB = u_ref.shape[0]
    R = NB * C
    quads = [slice(q * G, (q + 1) * G) for q in range(W // G)]
    seqs = [slice(e * C, (e + 1) * C) for e in range(NB)]

    @pl.when(pl.program_id(1) == 0)
    def _():
        state_ref[...] = jnp.zeros_like(state_ref)
        last_ref[...] = jnp.zeros_like(last_ref)

    u = u_ref[...].reshape(R, RWKV_IN)
    row = lax.broadcasted_iota(jnp.int32, (R, 1), 0)
    prev = pltpu.roll(u, 1, axis=0)
    for e in range(NB):
        prev = jnp.where(row == e * C, last_ref[e], prev)
        last_ref[e] = u[(e + 1) * C - 1:(e + 1) * C, :]
    u = u + (prev - u) * mu_ref[...]
    r, k, v = u[:, 0:W], u[:, W:2 * W], u[:, 2 * W:3 * W]
    lwa = u[:, 3 * W:]
    bd16 = bd16_ref[...]

    def segsum(x):
        return jnp.concatenate([_dot_exact_rhs(x[:, sl], bd16, parts=2) for sl in quads], axis=1)

    w = -jax.nn.softplus(-(w0_ref[...] + _dot_hi(jnp.tanh(lwa), wup_ref[...]))) - 0.5
    logd = -jnp.exp(w)
    a = jax.nn.sigmoid(a0_ref[...] + _dot_hi(lwa, aup_ref[...]))
    kk = k * kk_ref[...]
    kk = kk / jnp.maximum(jnp.sqrt(segsum(kk * kk)), 1e-12)
    k = k * (1.0 + (a - 1.0) * ka_ref[...])
    bonus = segsum(r * k * rk_ref[...]) * v

    tr = lax.broadcasted_iota(jnp.int32, (R, R), 0)
    tc = lax.broadcasted_iota(jnp.int32, (R, R), 1)
    tri = jnp.where((tc <= tr) & (tc >= (tr // C) * C), 1.0, 0.0).astype(BF16)
    b = _dot_exact_lhs(tri, logd)
    btot = jnp.concatenate([jnp.broadcast_to(b[sq.stop - 1:sq.stop, :], (C, W)) for sq in seqs], axis=0)
    enb = jnp.exp(-b)
    rt = r * jnp.exp(b)
    alt = -kk * jnp.exp(b - logd)
    beta = kk * a
    bt = beta * enb
    kt = k * enb
    edec = jnp.exp(btot - b)
    bhat = beta * edec
    khat = k * edec
    gams = [jnp.exp(b[sq.stop - 1:sq.stop, :]) for sq in seqs]

    t_idx = lax.broadcasted_iota(jnp.int32, (C, G), 0)
    s_idx = lax.broadcasted_iota(jnp.int32, (C, G), 1) % C
    incl = s_idx <= t_idx
    strict = s_idx < t_idx
    eye = jnp.where(s_idx == t_idx, 1.0, 0.0)

    chains = [(e, q, sq, sl) for e, sq in enumerate(seqs) for q, sl in enumerate(quads)]
    ars = [jnp.concatenate([alt[sq, sl], rt[sq, sl]], axis=0).astype(BF16) for _, _, sq, sl in chains]
    abs_ = [_dot_nt(ar, _expand(bt[sq, sl], bd16)) for ar, (_, _, sq, sl) in zip(ars, chains)]
    aks = [_dot_nt(ar, _expand(kt[sq, sl], bd16)) for ar, (_, _, sq, sl) in zip(ars, chains)]
    t16s = [t.astype(BF16) for t in _tri_inverse([jnp.where(strict, ab[:C], 0.0) for ab in abs_], eye, bd16)]
    m16s = [_dot(t16, _expand(jnp.where(strict, ak[:C], 0.0), bd16)).astype(BF16) for t16, ak in zip(t16s, aks)]
    a_rbs = [jnp.where(incl, ab[C:], 0.0).astype(BF16) for ab in abs_]
    a_rks = [jnp.where(incl, ak[C:], 0.0).astype(BF16) for ak in aks]
    evs = [_expand(v[sq, sl], bd16) for _, _, sq, sl in chains]
    hts = [state_ref[e, q] for e, q, _, _ in chains]
    prs = [_dot_nt(ar, ht.astype(BF16)) for ar, ht in zip(ars, hts)]
    us = [_dot(t16, _expand(pr[:C], bd16)) + _dot(m16, ev) for t16, pr, m16, ev in zip(t16s, prs, m16s, evs)]
    for (e, q, sq, sl), ht, uq in zip(chains, hts, us):
        z = jnp.concatenate([v[sq, sl], uq], axis=0).astype(BF16)
        wk = jnp.concatenate([khat[sq, sl], bhat[sq, sl]], axis=0).astype(BF16)
        state_ref[e, q] = ht * gams[e][:, sl] + bd_ref[...] * _dot_tn(z, wk)
    ys = [pr[C:] + _dot(a_rb, _expand(uq, bd16)) + _dot(a_rk, ev)
          for pr, a_rb, uq, a_rk, ev in zip(prs, a_rbs, us, a_rks, evs)]
    nq = len(quads)
    y = jnp.concatenate([jnp.concatenate(ys[e * nq:(e + 1) * nq], axis=1) for e in range(NB)], axis=0)
    mean = segsum(y) * (1.0 / RWKV_HEAD)
    d = y - mean
    var = segsum(d * d) * (1.0 / RWKV_HEAD)
    y = d * lax.rsqrt(var + RWKV_GN_EPS)
    o_ref[...] = (y * lnw_ref[...] + lnb_ref[...] + bonus).reshape(NB, C, W).astype(o_ref.dtype)


def _rwkv(p_rwkv, mu, w0, w_up, a0, a_up, k_k, k_a, r_k, ln_w, ln_b):
    B, S, _ = p_rwkv.shape
    C, W, G = RWKV_CHUNK, BRANCH_W, RWKV_QUAD
    NB = _tile(B, RWKV_SEQS)
    assert C == RWKV_HEAD
    wup = jnp.concatenate([w_up, jnp.zeros_like(a_up)], axis=0)
    aup = jnp.concatenate([jnp.zeros_like(w_up), a_up], axis=0)
    bd = _block_diag_ones(G, RWKV_HEAD)
    row = lambda n: pl.BlockSpec((1, n), lambda b, c: (0, 0))
    full = lambda a: pl.BlockSpec(a.shape, lambda b, c: (0,) * a.ndim)
    return pl.pallas_call(
        _rwkv_kernel,
        grid=(B // NB, S // C),
        in_specs=[pl.BlockSpec((NB, C, RWKV_IN), lambda b, c: (b, c, 0)),
                  row(RWKV_IN), row(W), full(wup), row(W), full(aup), row(W), row(W), row(W), row(W), row(W),
                  full(bd), full(bd)],
        out_specs=pl.BlockSpec((NB, C, W), lambda b, c: (b, c, 0)),
        out_shape=jax.ShapeDtypeStruct((B, S, W), BRANCH_DTYPE),
        scratch_shapes=[pltpu.VMEM((NB, W // G, G, G), F32), pltpu.VMEM((NB, 1, RWKV_IN), F32)],
        compiler_params=pltpu.CompilerParams(dimension_semantics=("parallel", "arbitrary"),
                                             vmem_limit_bytes=VMEM_LIMIT),
        name="rwkv7",
    )(p_rwkv, mu, w0, wup, a0, aup, k_k, k_a, r_k.reshape(1, W), ln_w, ln_b, bd, bd.astype(BF16))


def _gla_kernel(p_ref, aup_ref, ab_ref, gn_ref, seg_ref, bdk_ref, bd_ref, o_ref, state_ref, *, nchunk):
    C = GLA_CHUNK
    c = GLA_BLOCK
    nb = C // c
    KW = GLA_HEADS * GLA_DK
    VW = GLA_HEADS * GLA_DV
    reps = GLA_HEADS

    @pl.when(pl.program_id(1) == 0)
    def _():
        state_ref[...] = jnp.zeros_like(state_ref)

    tr = lax.broadcasted_iota(jnp.int32, (C, C), 0)
    tc = lax.broadcasted_iota(jnp.int32, (C, C), 1)
    tri = jnp.where(tc <= tr, 1.0, 0.0).astype(BF16)
    row_c = lax.broadcasted_iota(jnp.int32, (c, 1), 0)
    row_C = lax.broadcasted_iota(jnp.int32, (C, 1), 0)

    def chunk(i, carry):
        rows = pl.ds(pl.multiple_of(i * C, C), C)
        q = p_ref[rows, 0:KW] * (GLA_DK ** -0.5)
        k = p_ref[rows, KW:2 * KW]
        v = p_ref[rows, 2 * KW:2 * KW + VW]
        lat = p_ref[rows, 2 * KW + VW:]
        z = _dot_hi(lat, aup_ref[...]) + ab_ref[...]
        log_a = jax.nn.log_sigmoid(z) * (1.0 / GLA_TAU)
        b = _dot_exact_lhs(tri, log_a)
        ht = state_ref[...]
        o_inter = _dot_nt((q * jnp.exp(b)).astype(BF16), ht.astype(BF16))

        v16 = v.astype(BF16)
        ev = jnp.concatenate([v16] * reps, axis=0) * seg_ref[...]
        atts = []
        for blk in range(1, nb):
            lo = blk * c
            m = b[lo - 1:lo, :]
            qb = q[lo:lo + c] * jnp.exp(b[lo:lo + c] - m)
            kb = (k * jnp.exp(jnp.where(row_C < lo, m - b, -jnp.inf))).astype(BF16)
            ke = jnp.concatenate([kb] * reps, axis=0) * bdk_ref[...]
            atts.append(_dot_nt(qb.astype(BF16), ke))
        o_off = _dot(jnp.concatenate(atts, axis=0).astype(BF16), ev)

        outs = []
        for blk in range(nb):
            lo = blk * c
            bb, qq, kk, vv = b[lo:lo + c], q[lo:lo + c], k[lo:lo + c], v[lo:lo + c]
            xs = []
            for j in range(c):
                diff = jnp.where(row_c >= j, bb - bb[j:j + 1, :], -jnp.inf)
                xs.append(qq * jnp.exp(diff) * kk[j:j + 1, :])
            att = _dot(jnp.concatenate(xs, axis=0).astype(BF16), seg_ref[...])
            o = o_inter[lo:lo + c]
            if blk:
                o = o + o_off[lo - c:lo]
            for j in range(c):
                o = o + att[j * c:(j + 1) * c, :] * vv[j:j + 1, :]
            outs.append(o)
        o = jnp.concatenate(outs, axis=0)

        blast = b[C - 1:C, :]
        khat = (k * jnp.exp(blast - b)).astype(BF16)
        state_ref[...] = ht * jnp.exp(blast) + bd_ref[...] * _dot_tn(v16, khat)
        heads = []
        for h in range(GLA_HEADS):
            hs = slice(h * GLA_DV, (h + 1) * GLA_DV)
            heads.append(_rms(o[:, hs], gn_ref[:, hs]))
        o_ref[rows, :] = jnp.concatenate(heads, axis=1).astype(o_ref.dtype)
        return carry

    lax.fori_loop(0, nchunk, chunk, 0)


def _gla(p_gla, a_up, a_b, g_norm):
    B, S, _ = p_gla.shape
    KW = GLA_HEADS * GLA_DK
    VW = GLA_HEADS * GLA_DV
    tg = _tile(S, TILE_ROWS["gla"])
    assert GLA_CHUNK == GLA_DK and tg % GLA_CHUNK == 0
    aup = jnp.concatenate([a_up, jnp.zeros((GLA_IN_W - 2 * KW - VW - GLA_GATE_RANK, KW), a_up.dtype)], axis=0)
    seg = _block_diag_ones(KW, GLA_DK, GLA_DV, VW, dtype=BF16)
    bdk = _block_diag_ones(KW, GLA_DK, dtype=BF16)
    bd = _block_diag_ones(VW, GLA_DV, GLA_DK, KW)
    full = lambda a: pl.BlockSpec(a.shape, lambda b, t: (0,) * a.ndim)
    return pl.pallas_call(
        functools.partial(_gla_kernel, nchunk=tg // GLA_CHUNK),
        grid=(B, S // tg),
        in_specs=[pl.BlockSpec((None, tg, GLA_IN_W), lambda b, t: (b, t, 0)),
                  full(aup), pl.BlockSpec((1, KW), lambda b, t: (0, 0)), pl.BlockSpec((1, VW), lambda b, t: (0, 0)),
                  full(seg), full(bdk), full(bd)],
        out_specs=pl.BlockSpec((None, tg, VW), lambda b, t: (b, t, 0)),
        out_shape=jax.ShapeDtypeStruct((B, S, VW), BRANCH_DTYPE),
        scratch_shapes=[pltpu.VMEM((VW, KW), F32)],
        compiler_params=pltpu.CompilerParams(dimension_semantics=("parallel", "arbitrary"),
                                             vmem_limit_bytes=VMEM_LIMIT),
        name="gla",
    )(p_gla, aup, a_b, g_norm, seg, bdk, bd)


def _merge_kernel(x_ref, ym_ref, yr_ref, yg_ref, gate_ref, mg_ref, wb_ref, wo_ref, gpost_ref, o_ref):
    W = BRANCH_W
    merged = None
    for n, y_ref in enumerate((ym_ref, yr_ref, yg_ref)):
        ys = (y_ref[...].astype(F32) * jax.nn.silu(gate_ref[:, n * W:(n + 1) * W].astype(F32))).astype(BF16)
        br = _dot(ys, wb_ref[n]) * jax.nn.sigmoid(mg_ref[:, n * D_MODEL:(n + 1) * D_MODEL].astype(F32))
        merged = br if merged is None else merged + br
    out = _dot(merged.astype(BF16), wo_ref[...])
    o_ref[...] = x_ref[...] + _rms(out, gpost_ref[...])


def _merge(x2, ym, yr, yg, gate, mg, wb, wo, gpost):
    N = x2.shape[0]
    W = BRANCH_W
    tm = _tile(N, TILE_ROWS["merge"])
    tile = lambda n: pl.BlockSpec((tm, n), lambda i: (i, 0))
    return pl.pallas_call(
        _merge_kernel,
        grid=(N // tm,),
        in_specs=[tile(D_MODEL), tile(W), tile(W), tile(W), tile(GATE_W), tile(MERGE_W),
                  pl.BlockSpec((N_BRANCH, W, D_MODEL), lambda i: (0, 0, 0)),
                  pl.BlockSpec((D_MODEL, D_MODEL), lambda i: (0, 0)),
                  pl.BlockSpec((1, D_MODEL), lambda i: (0, 0))],
        out_specs=tile(D_MODEL),
        out_shape=jax.ShapeDtypeStruct((N, D_MODEL), F32),
        compiler_params=pltpu.CompilerParams(dimension_semantics=("parallel",), vmem_limit_bytes=VMEM_LIMIT),
        name="merge_out",
    )(x2, ym, yr, yg, gate, mg, wb, wo, gpost)


def kernel(x, positions, norm_pre, w_in, mla_q_norm, mla_kv_norm, mla_w_uq, mla_w_ukv, rwkv_mu, rwkv_w0,
           rwkv_w_up, rwkv_a0, rwkv_a_up, rwkv_k_k, rwkv_k_a, rwkv_r_k, rwkv_ln_w, rwkv_ln_b, gla_a_up, gla_a_b,
           gla_norm, w_branch_out, w_out, norm_post):
    B, S, D = x.shape
    assert D == D_MODEL and S % RWKV_CHUNK == 0
    depth = w_in.shape[0]
    N = B * S
    cosm, sinm = _rope_tables(positions)
    x2 = x.reshape(N, D)
    row = lambda a: a.reshape(1, -1)
    for l in range(depth):
        p_mla, p_rwkv, p_gla, p_gate, p_merge = _inproj(x2, row(norm_pre[l]), _prep_w_in(w_in[l]))
        wq, wqr, wk, wv = _prep_mla_weights(mla_w_uq[l], mla_w_ukv[l])
        q, k, v = _mla_prep(p_mla.reshape(B, S, -1), cosm, sinm, row(mla_q_norm[l]), row(mla_kv_norm[l]),
                            wq, wqr, wk, wv)
        y_mla = _mla_attn(q, k, v)
        y_rwkv = _rwkv(p_rwkv.reshape(B, S, -1), row(rwkv_mu[l]), row(rwkv_w0[l]), rwkv_w_up[l], row(rwkv_a0[l]),
                       rwkv_a_up[l], row(rwkv_k_k[l]), row(rwkv_k_a[l]), rwkv_r_k[l], row(rwkv_ln_w[l]),
                       row(rwkv_ln_b[l]))
        y_gla = _gla(p_gla.reshape(B, S, -1), gla_a_up[l], row(gla_a_b[l]), row(gla_norm[l]))
        x2 = _merge(x2, y_mla.reshape(N, -1), y_rwkv.reshape(N, -1), y_gla.reshape(N, -1), p_gate, p_merge,
                    w_branch_out[l].astype(BF16), w_out[l].astype(BF16), row(norm_post[l]))
    return x2.reshape(B, S, D)
```

```python
import functools

import jax
import jax.numpy as jnp
from jax import lax
from jax.experimental import pallas as pl
from jax.experimental.pallas import tpu as pltpu

F32 = jnp.float32
BF16 = jnp.bfloat16

D_MODEL = 1024
N_BRANCH = 3
BRANCH_W = D_MODEL // 2
NORM_EPS = 1e-6
MLA_HEADS = 8
MLA_NOPE = 64
MLA_ROPE = 32
MLA_V = BRANCH_W // MLA_HEADS
MLA_Q_LORA = 256
MLA_KV_LORA = 128
ROPE_BASE = 10000.0
RWKV_HEADS = 8
RWKV_HEAD = BRANCH_W // RWKV_HEADS
RWKV_DECAY_RANK = 64
RWKV_ICLR_RANK = 64
RWKV_IN = 3 * BRANCH_W + RWKV_DECAY_RANK + RWKV_ICLR_RANK
RWKV_GN_EPS = 64e-5
GLA_HEADS = 4
GLA_DK = 64
GLA_DV = BRANCH_W // GLA_HEADS
GLA_GATE_RANK = 16
GLA_TAU = 16.0
IN_SIZES = (MLA_Q_LORA, MLA_KV_LORA, MLA_ROPE, RWKV_IN,
            GLA_HEADS * GLA_DK, GLA_HEADS * GLA_DK, GLA_HEADS * GLA_DV, GLA_GATE_RANK,
            N_BRANCH * BRANCH_W, N_BRANCH * D_MODEL)

LANE = 128
MLA_IN_W = 640
GLA_IN_W = 1152
GATE_W = N_BRANCH * BRANCH_W
MERGE_W = N_BRANCH * D_MODEL
PROJ_W = MLA_IN_W + RWKV_IN + GLA_IN_W + GATE_W + MERGE_W
QK_W = 128

RWKV_CHUNK = 64
RWKV_QUAD = 256
RWKV_SEQS = 4
GLA_CHUNK = 64
GLA_BLOCK = 16
VMEM_LIMIT = 48 * 1024 * 1024

TILE_ROWS = dict(rope=512, inproj=512, mla_prep=512, attn=512, gla=256, merge=512)


def _tile(n, want):
    t = min(n, want)
    assert n % t == 0, (n, t)
    return t


def _dot(a, b):
    return jnp.dot(a, b, preferred_element_type=F32)


def _dot_nt(a, b):
    return lax.dot_general(a, b, (((1,), (1,)), ((), ())), preferred_element_type=F32)


def _dot_tn(a, b):
    return lax.dot_general(a, b, (((0,), (0,)), ((), ())), preferred_element_type=F32)


def _split(x, parts):
    out = []
    rem = x
    for _ in range(parts):
        t = rem.astype(BF16)
        out.append(t)
        rem = rem - t.astype(F32)
    return out


def _dot_exact_rhs(a, b_bf16, parts=3):
    acc = None
    for t in _split(a, parts):
        d = _dot(t, b_bf16)
        acc = d if acc is None else acc + d
    return acc


def _dot_exact_lhs(a_bf16, b, parts=3):
    acc = None
    for t in _split(b, parts):
        d = _dot(a_bf16, t)
        acc = d if acc is None else acc + d
    return acc


def _dot_hi(a, b):
    a1, a2 = _split(a, 2)
    b1, b2 = _split(b, 2)
    return _dot(a1, b1) + (_dot(a1, b2) + _dot(a2, b1))


def _rms(x, g):
    return x * lax.rsqrt(jnp.mean(x * x, axis=-1, keepdims=True) + NORM_EPS) * g


def _block_diag_ones(n, blk_r, blk_c=None, m=None, dtype=F32):
    blk_c = blk_r if blk_c is None else blk_c
    m = n if m is None else m
    r = jnp.arange(n)[:, None] // blk_r
    c = jnp.arange(m)[None, :] // blk_c
    return (r == c).astype(dtype)


def _rope_kernel(pos_ref, inv_ref, cos_ref, sin_ref):
    ang = pos_ref[...] * inv_ref[...]
    lane = lax.broadcasted_iota(jnp.int32, ang.shape, 1)
    rope = (lane >= MLA_NOPE) & (lane < MLA_NOPE + MLA_ROPE)
    cos_ref[...] = jnp.where(rope, jnp.cos(ang), jnp.where(lane < MLA_NOPE, 1.0, 0.0))
    sin_ref[...] = jnp.where(rope, jnp.sin(ang), 0.0)


def _rope_tables(positions):
    B, S = positions.shape
    ts = _tile(S, TILE_ROWS["rope"])
    inv = 1.0 / (ROPE_BASE ** (jnp.arange(0, MLA_ROPE, 2, dtype=F32) / MLA_ROPE))
    inv_row = jnp.zeros((1, QK_W), F32).at[0, MLA_NOPE:MLA_NOPE + MLA_ROPE].set(jnp.tile(inv, 2))
    pos = positions.astype(F32)[..., None]
    return pl.pallas_call(
        _rope_kernel,
        grid=(B, S // ts),
        in_specs=[pl.BlockSpec((None, ts, 1), lambda b, i: (b, i, 0)),
                  pl.BlockSpec((1, QK_W), lambda b, i: (0, 0))],
        out_specs=[pl.BlockSpec((None, ts, QK_W), lambda b, i: (b, i, 0))] * 2,
        out_shape=[jax.ShapeDtypeStruct((B, S, QK_W), F32)] * 2,
        name="rope_tables",
    )(pos, inv_row)


_PROJ_SPLITS = (MLA_IN_W, RWKV_IN, GLA_IN_W, GATE_W, MERGE_W)
_PROJ_DTYPES = (F32, F32, F32, BF16, BF16)
BRANCH_DTYPE = BF16


def _inproj_kernel(x_ref, g_ref, w_ref, mla_ref, rwkv_ref, gla_ref, gate_ref, merge_ref):
    h = _rms(x_ref[...], g_ref[...]).astype(BF16)
    off = 0
    for ref, n in zip((mla_ref, rwkv_ref, gla_ref, gate_ref, merge_ref), _PROJ_SPLITS):
        ref[...] = _dot(h, w_ref[:, off:off + n]).astype(ref.dtype)
        off += n


def _inproj(x2, g, w):
    N = x2.shape[0]
    tm = _tile(N, TILE_ROWS["inproj"])
    return pl.pallas_call(
        _inproj_kernel,
        grid=(N // tm,),
        in_specs=[pl.BlockSpec((tm, D_MODEL), lambda i: (i, 0)),
                  pl.BlockSpec((1, D_MODEL), lambda i: (0, 0)),
                  pl.BlockSpec((D_MODEL, PROJ_W), lambda i: (0, 0), pipeline_mode=pl.Buffered(1))],
        out_specs=[pl.BlockSpec((tm, n), lambda i: (i, 0)) for n in _PROJ_SPLITS],
        out_shape=[jax.ShapeDtypeStruct((N, n), dt) for n, dt in zip(_PROJ_SPLITS, _PROJ_DTYPES)],
        compiler_params=pltpu.CompilerParams(dimension_semantics=("parallel",), vmem_limit_bytes=VMEM_LIMIT),
        name="inproj",
    )(x2, g, w)


def _prep_w_in(w_in):
    offs = [0]
    for n in IN_SIZES:
        offs.append(offs[-1] + n)
    c_q, c_kv, k_rope, u_rwkv, g_q, g_k, g_v, g_lat, br_gate, merge_gate = (
        w_in[:, offs[i]:offs[i + 1]] for i in range(len(IN_SIZES)))
    z = lambda n: jnp.zeros((D_MODEL, n), w_in.dtype)
    half = MLA_ROPE // 2
    k_rot = jnp.concatenate([-k_rope[:, half:], k_rope[:, :half]], axis=1)
    cols = [c_q, c_kv,
            z(MLA_NOPE), k_rope, z(QK_W - MLA_NOPE - MLA_ROPE),
            z(MLA_NOPE), k_rot, z(QK_W - MLA_NOPE - MLA_ROPE),
            u_rwkv,
            g_q, g_k, g_v, g_lat, z(GLA_IN_W - 1024 - GLA_GATE_RANK),
            br_gate, merge_gate]
    return jnp.concatenate(cols, axis=1).astype(BF16)


def _mla_prep_kernel(p_ref, cos_ref, sin_ref, qn_ref, kvn_ref, wq_ref, wqr_ref, wk_ref, wv_ref,
                     q_ref, k_ref, v_ref):
    cm = cos_ref[...]
    sm = sin_ref[...]
    scale = (MLA_NOPE + MLA_ROPE) ** -0.5
    cq = _rms(p_ref[:, 0:MLA_Q_LORA], qn_ref[...]).astype(BF16)
    q_all = _dot(cq, wq_ref[...])
    q_rot = _dot(cq, wqr_ref[...])
    ckv = _rms(p_ref[:, MLA_Q_LORA:MLA_Q_LORA + MLA_KV_LORA], kvn_ref[...]).astype(BF16)
    k_all = _dot(ckv, wk_ref[...])
    v_all = _dot(ckv, wv_ref[...])
    o = MLA_Q_LORA + MLA_KV_LORA
    k_r = p_ref[:, o:o + QK_W] * cm + p_ref[:, o + QK_W:o + 2 * QK_W] * sm
    for h in range(MLA_HEADS):
        sl = slice(h * QK_W, (h + 1) * QK_W)
        q_ref[h] = ((q_all[:, sl] * cm + q_rot[:, sl] * sm) * scale).astype(q_ref.dtype)
        k_ref[h] = (k_all[:, sl] + k_r).astype(k_ref.dtype)
    for hp in range(MLA_HEADS // 2):
        v_ref[hp] = v_all[:, hp * LANE:(hp + 1) * LANE].astype(v_ref.dtype)


def _prep_mla_weights(w_uq, w_ukv):
    H = MLA_HEADS
    half = MLA_ROPE // 2
    wq = w_uq.reshape(MLA_Q_LORA, H, MLA_NOPE + MLA_ROPE)
    nope, rope = wq[..., :MLA_NOPE], wq[..., MLA_NOPE:]
    rot = jnp.concatenate([-rope[..., half:], rope[..., :half]], axis=-1)
    zq = lambda n: jnp.zeros((MLA_Q_LORA, H, n), w_uq.dtype)
    pad = QK_W - MLA_NOPE - MLA_ROPE
    wq_main = jnp.concatenate([nope, rope, zq(pad)], axis=-1).reshape(MLA_Q_LORA, H * QK_W)
    wq_rot = jnp.concatenate([zq(MLA_NOPE), rot, zq(pad)], axis=-1).reshape(MLA_Q_LORA, H * QK_W)
    wkv = w_ukv.reshape(MLA_KV_LORA, H, MLA_NOPE + MLA_V)
    wk = jnp.concatenate([wkv[..., :MLA_NOPE], jnp.zeros((MLA_KV_LORA, H, QK_W - MLA_NOPE), w_ukv.dtype)],
                         axis=-1).reshape(MLA_KV_LORA, H * QK_W)
    wv = wkv[..., MLA_NOPE:].reshape(MLA_KV_LORA, H * MLA_V)
    return wq_main.astype(BF16), wq_rot.astype(BF16), wk.astype(BF16), wv.astype(BF16)


def _mla_prep(p_mla, cosm, sinm, q_norm, kv_norm, wq, wqr, wk, wv):
    B, S, _ = p_mla.shape
    H = MLA_HEADS
    ts = _tile(S, TILE_ROWS["mla_prep"])
    const = lambda shape: pl.BlockSpec(shape, lambda b, i: (0,) * len(shape))
    return pl.pallas_call(
        _mla_prep_kernel,
        grid=(B, S // ts),
        in_specs=[pl.BlockSpec((None, ts, MLA_IN_W), lambda b, i: (b, i, 0)),
                  pl.BlockSpec((None, ts, QK_W), lambda b, i: (b, i, 0)),
                  pl.BlockSpec((None, ts, QK_W), lambda b, i: (b, i, 0)),
                  const((1, MLA_Q_LORA)), const((1, MLA_KV_LORA)),
                  const(wq.shape), const(wqr.shape), const(wk.shape), const(wv.shape)],
        out_specs=[pl.BlockSpec((None, H, ts, QK_W), lambda b, i: (b, 0, i, 0)),
                   pl.BlockSpec((None, H, ts, QK_W), lambda b, i: (b, 0, i, 0)),
                   pl.BlockSpec((None, H // 2, ts, LANE), lambda b, i: (b, 0, i, 0))],
        out_shape=[jax.ShapeDtypeStruct((B, H, S, QK_W), BF16),
                   jax.ShapeDtypeStruct((B, H, S, QK_W), BF16),
                   jax.ShapeDtypeStruct((B, H // 2, S, LANE), BF16)],
        compiler_params=pltpu.CompilerParams(dimension_semantics=("parallel", "parallel"),
                                             vmem_limit_bytes=VMEM_LIMIT),
        name="mla_prep",
    )(p_mla, cosm, sinm, q_norm, kv_norm, wq, wqr, wk, wv)


def _attn_kernel(q_ref, k_ref, v_ref, o_ref, acc_ref, s_ref, *, tq):
    qi = pl.program_id(2)
    lane = lax.broadcasted_iota(jnp.int32, (1, LANE), 1)
    first = lane < MLA_V
    acc_ref[...] = jnp.zeros_like(acc_ref)

    def scores(j, slot):
        rows = pl.ds(pl.multiple_of(j * tq, tq), tq)
        for e in range(2):
            s_ref[slot, e] = _dot_nt(q_ref[e], k_ref[e, rows, :])

    def consume(j, slot, carry, masked):
        m0, l0, m1, l1 = carry
        rows = pl.ds(pl.multiple_of(j * tq, tq), tq)
        v = v_ref[rows, :]
        zero = jnp.zeros_like(v)
        outs = []
        for e, m, l in ((0, m0, l0), (1, m1, l1)):
            s = s_ref[slot, e]
            if masked:
                r = lax.broadcasted_iota(jnp.int32, s.shape, 0)
                c = lax.broadcasted_iota(jnp.int32, s.shape, 1)
                s = jnp.where(c <= r, s, -jnp.inf)
            m_new = jnp.maximum(m, jnp.max(s, axis=-1, keepdims=True))
            p = jnp.exp(s - m_new)
            alpha = jnp.exp(m - m_new)
            l_new = alpha * l + jnp.sum(p, axis=-1, keepdims=True)
            ve = jnp.where(first, v, zero) if e == 0 else jnp.where(first, zero, v)
            outs.append((m_new, l_new, alpha, _dot(p.astype(v.dtype), ve)))
        (m0, l0, a0, pv0), (m1, l1, a1, pv1) = outs
        acc_ref[...] = acc_ref[...] * jnp.where(first, a0, a1) + (pv0 + pv1)
        return m0, l0, m1, l1

    def pair(i, carry):
        scores(2 * i + 1, 1)
        carry = consume(2 * i, 0, carry, False)
        scores(2 * i + 2, 0)
        return consume(2 * i + 1, 1, carry, False)

    def even_tail(carry):
        return consume(qi, 0, carry, True)

    def odd_tail(carry):
        scores(qi, 1)
        return consume(qi, 1, consume(qi - 1, 0, carry, False), True)

    neg = jnp.full((tq, 1), -jnp.inf, F32)
    zer = jnp.zeros((tq, 1), F32)
    scores(0, 0)
    carry = lax.fori_loop(0, qi // 2, pair, (neg, zer, neg, zer))
    _, l0, _, l1 = lax.cond(qi % 2 == 0, even_tail, odd_tail, carry)
    o_ref[...] = (acc_ref[...] / jnp.where(first, l0, l1)).astype(o_ref.dtype)


def _mla_attn(q, k, v):
    B, H, S, _ = q.shape
    tq = _tile(S, TILE_ROWS["attn"])
    return pl.pallas_call(
        functools.partial(_attn_kernel, tq=tq),
        grid=(B, H // 2, S // tq),
        in_specs=[pl.BlockSpec((None, 2, tq, QK_W), lambda b, h, i: (b, h, i, 0)),
                  pl.BlockSpec((None, 2, S, QK_W), lambda b, h, i: (b, h, 0, 0)),
                  pl.BlockSpec((None, None, S, LANE), lambda b, h, i: (b, h, 0, 0))],
        out_specs=pl.BlockSpec((None, tq, LANE), lambda b, h, i: (b, i, h)),
        out_shape=jax.ShapeDtypeStruct((B, S, BRANCH_W), BRANCH_DTYPE),
        scratch_shapes=[pltpu.VMEM((tq, LANE), F32), pltpu.VMEM((2, 2, tq, tq), F32)],
        compiler_params=pltpu.CompilerParams(dimension_semantics=("parallel", "parallel", "arbitrary"),
                                             vmem_limit_bytes=VMEM_LIMIT),
        name="mla_attn",
    )(q, k, v)


def _expand(x, bd16):
    x16 = x.astype(BF16)
    C, G = x.shape
    zero = jnp.zeros((C, LANE), BF16)
    blocks = []
    for h in range(bd16.shape[0] // C):
        lo = (h * RWKV_HEAD // LANE) * LANE
        own = x16[:, lo:lo + LANE] * bd16[h * C:(h + 1) * C, lo:lo + LANE]
        blocks.append(jnp.concatenate([own if l0 == lo else zero for l0 in range(0, G, LANE)], axis=1))
    return jnp.concatenate(blocks, axis=0)


def _tri_inverse(ns, eye, bd16):
    C = ns[0].shape[0]
    ps = [eye + n for n in ns]
    ss = [_dot(n.astype(BF16), _expand(n, bd16)) for n in ns]
    for _ in range(C.bit_length() - 3):
        sps = [_dot(jnp.concatenate([s, p], axis=0).astype(BF16), _expand(s, bd16)) for s, p in zip(ss, ps)]
        ss = [sp[:C] for sp in sps]
        ps = [p + sp[C:] for p, sp in zip(ps, sps)]
    return [p + _dot(p.astype(BF16), _expand(s, bd16)) for s, p in zip(ss, ps)]


def _rwkv_kernel(u_ref, mu_ref, w0_ref, wup_ref, a0_ref, aup_ref, kk_ref, ka_ref, rk_ref, lnw_ref, lnb_ref,
                 bd_ref, bd16_ref, o_ref, state_ref, last_ref):
    C = RWKV_CHUNK
    W = BRANCH_W
    G = RWKV_QUAD
    NB = u_ref.shape[0]
    R = NB * C
    quads = [slice(q * G, (q + 1) * G) for q in range(W // G)]
    seqs = [slice(e * C, (e + 1) * C) for e in range(NB)]

    @pl.when(pl.program_id(1) == 0)
    def _():
        state_ref[...] = jnp.zeros_like(state_ref)
        last_ref[...] = jnp.zeros_like(last_ref)

    u = u_ref[...].reshape(R, RWKV_IN)
    row = lax.broadcasted_iota(jnp.int32, (R, 1), 0)
    prev = pltpu.roll(u, 1, axis=0)
    for e in range(NB):
        prev = jnp.where(row == e * C, last_ref[e], prev)
        last_ref[e] = u[(e + 1) * C - 1:(e + 1) * C, :]
    u = u + (prev - u) * mu_ref[...]
    r, k, v = u[:, 0:W], u[:, W:2 * W], u[:, 2 * W:3 * W]
    lwa = u[:, 3 * W:]
    bd16 = bd16_ref[...]

    def segsum(x):
        return jnp.concatenate([_dot_exact_rhs(x[:, sl], bd16, parts=2) for sl in quads], axis=1)

    w = -jax.nn.softplus(-(w0_ref[...] + _dot_hi(jnp.tanh(lwa), wup_ref[...]))) - 0.5
    logd = -jnp.exp(w)
    a = jax.nn.sigmoid(a0_ref[...] + _dot_hi(lwa, aup_ref[...]))
    kk = k * kk_ref[...]
    kk = kk / jnp.maximum(jnp.sqrt(segsum(kk * kk)), 1e-12)
    k = k * (1.0 + (a - 1.0) * ka_ref[...])
    bonus = segsum(r * k * rk_ref[...]) * v

    tr = lax.broadcasted_iota(jnp.int32, (R, R), 0)
    tc = lax.broadcasted_iota(jnp.int32, (R, R), 1)
    tri = jnp.where((tc <= tr) & (tc >= (tr // C) * C), 1.0, 0.0).astype(BF16)
    b = _dot_exact_lhs(tri, logd)
    btot = jnp.concatenate([jnp.broadcast_to(b[sq.stop - 1:sq.stop, :], (C, W)) for sq in seqs], axis=0)
    enb = jnp.exp(-b)
    rt = r * jnp.exp(b)
    alt = -kk * jnp.exp(b - logd)
    beta = kk * a
    bt = beta * enb
    kt = k * enb
    edec = jnp.exp(btot - b)
    bhat = beta * edec
    khat = k * edec
    gams = [jnp.exp(b[sq.stop - 1:sq.stop, :]) for sq in seqs]

    t_idx = lax.broadcasted_iota(jnp.int32, (C, G), 0)
    s_idx = lax.broadcasted_iota(jnp.int32, (C, G), 1) % C
    incl = s_idx <= t_idx
    strict = s_idx < t_idx
    eye = jnp.where(s_idx == t_idx, 1.0, 0.0)

    chains = [(e, q, sq, sl) for e, sq in enumerate(seqs) for q, sl in enumerate(quads)]
    ars = [jnp.concatenate([alt[sq, sl], rt[sq, sl]], axis=0).astype(BF16) for _, _, sq, sl in chains]
    abs_ = [_dot_nt(ar, _expand(bt[sq, sl], bd16)) for ar, (_, _, sq, sl) in zip(ars, chains)]
    aks = [_dot_nt(ar, _expand(kt[sq, sl], bd16)) for ar, (_, _, sq, sl) in zip(ars, chains)]
    t16s = [t.astype(BF16) for t in _tri_inverse([jnp.where(strict, ab[:C], 0.0) for ab in abs_], eye, bd16)]
    m16s = [_dot(t16, _expand(jnp.where(strict, ak[:C], 0.0), bd16)).astype(BF16) for t16, ak in zip(t16s, aks)]
    a_rbs = [jnp.where(incl, ab[C:], 0.0).astype(BF16) for ab in abs_]
    a_rks = [jnp.where(incl, ak[C:], 0.0).astype(BF16) for ak in aks]
    evs = [_expand(v[sq, sl], bd16) for _, _, sq, sl in chains]
    hts = [state_ref[e, q] for e, q, _, _ in chains]
    prs = [_dot_nt(ar, ht.astype(BF16)) for ar, ht in zip(ars, hts)]
    us = [_dot(t16, _expand(pr[:C], bd16)) + _dot(m16, ev) for t16, pr, m16, ev in zip(t16s, prs, m16s, evs)]
    for (e, q, sq, sl), ht, uq in zip(chains, hts, us):
        z = jnp.concatenate([v[sq, sl], uq], axis=0).astype(BF16)
        wk = jnp.concatenate([khat[sq, sl], bhat[sq, sl]], axis=0).astype(BF16)
        state_ref[e, q] = ht * gams[e][:, sl] + bd_ref[...] * _dot_tn(z, wk)
    ys = [pr[C:] + _dot(a_rb, _expand(uq, bd16)) + _dot(a_rk, ev)
          for pr, a_rb, uq, a_rk, ev in zip(prs, a_rbs, us, a_rks, evs)]
    nq = len(quads)
    y = jnp.concatenate([jnp.concatenate(ys[e * nq:(e + 1) * nq], axis=1) for e in range(NB)], axis=0)
    mean = segsum(y) * (1.0 / RWKV_HEAD)
    d = y - mean
    var = segsum(d * d) * (1.0 / RWKV_HEAD)
    y = d * lax.rsqrt(var + RWKV_GN_EPS)
    o_ref[...] = (y * lnw_ref[...] + lnb_ref[...] + bonus).reshape(NB, C, W).astype(o_ref.dtype)


def _rwkv(p_rwkv, mu, w0, w_up, a0, a_up, k_k, k_a, r_k, ln_w, ln_b):
    B, S, _ = p_rwkv.shape
    C, W, G = RWKV_CHUNK, BRANCH_W, RWKV_QUAD
    NB = _tile(B, RWKV_SEQS)
    assert C == RWKV_HEAD
    wup = jnp.concatenate([w_up, jnp.zeros_like(a_up)], axis=0)
    aup = jnp.concatenate([jnp.zeros_like(w_up), a_up], axis=0)
    bd = _block_diag_ones(G, RWKV_HEAD)
    row = lambda n: pl.BlockSpec((1, n), lambda b, c: (0, 0))
    full = lambda a: pl.BlockSpec(a.shape, lambda b, c: (0,) * a.ndim)
    return pl.pallas_call(
        _rwkv_kernel,
        grid=(B // NB, S // C),
        in_specs=[pl.BlockSpec((NB, C, RWKV_IN), lambda b, c: (b, c, 0)),
                  row(RWKV_IN), row(W), full(wup), row(W), full(aup), row(W), row(W), row(W), row(W), row(W),
                  full(bd), full(bd)],
        out_specs=pl.BlockSpec((NB, C, W), lambda b, c: (b, c, 0)),
        out_shape=jax.ShapeDtypeStruct((B, S, W), BRANCH_DTYPE),
        scratch_shapes=[pltpu.VMEM((NB, W // G, G, G), F32), pltpu.VMEM((NB, 1, RWKV_IN), F32)],
        compiler_params=pltpu.CompilerParams(dimension_semantics=("parallel", "arbitrary"),
                                             vmem_limit_bytes=VMEM_LIMIT),
        name="rwkv7",
    )(p_rwkv, mu, w0, wup, a0, aup, k_k, k_a, r_k.reshape(1, W), ln_w, ln_b, bd, bd.astype(BF16))


def _gla_kernel(p_ref, aup_ref, ab_ref, gn_ref, seg_ref, bdk_ref, bd_ref, o_ref, state_ref, *, nchunk):
    C = GLA_CHUNK
    c = GLA_BLOCK
    nb = C // c
    KW = GLA_HEADS * GLA_DK
    VW = GLA_HEADS * GLA_DV
    reps = GLA_HEADS

    @pl.when(pl.program_id(1) == 0)
    def _():
        state_ref[...] = jnp.zeros_like(state_ref)

    tr = lax.broadcasted_iota(jnp.int32, (C, C), 0)
    tc = lax.broadcasted_iota(jnp.int32, (C, C), 1)
    tri = jnp.where(tc <= tr, 1.0, 0.0).astype(BF16)
    row_c = lax.broadcasted_iota(jnp.int32, (c, 1), 0)
    row_C = lax.broadcasted_iota(jnp.int32, (C, 1), 0)

    def chunk(i, carry):
        rows = pl.ds(pl.multiple_of(i * C, C), C)
        q = p_ref[rows, 0:KW] * (GLA_DK ** -0.5)
        k = p_ref[rows, KW:2 * KW]
        v = p_ref[rows, 2 * KW:2 * KW + VW]
        lat = p_ref[rows, 2 * KW + VW:]
        z = _dot_hi(lat, aup_ref[...]) + ab_ref[...]
        log_a = jax.nn.log_sigmoid(z) * (1.0 / GLA_TAU)
        b = _dot_exact_lhs(tri, log_a)
        ht = state_ref[...]
        o_inter = _dot_nt((q * jnp.exp(b)).astype(BF16), ht.astype(BF16))

        v16 = v.astype(BF16)
        ev = jnp.concatenate([v16] * reps, axis=0) * seg_ref[...]
        atts = []
        for blk in range(1, nb):
            lo = blk * c
            m = b[lo - 1:lo, :]
            qb = q[lo:lo + c] * jnp.exp(b[lo:lo + c] - m)
            kb = (k * jnp.exp(jnp.where(row_C < lo, m - b, -jnp.inf))).astype(BF16)
            ke = jnp.concatenate([kb] * reps, axis=0) * bdk_ref[...]
            atts.append(_dot_nt(qb.astype(BF16), ke))
        o_off = _dot(jnp.concatenate(atts, axis=0).astype(BF16), ev)

        outs = []
        for blk in range(nb):
            lo = blk * c
            bb, qq, kk, vv = b[lo:lo + c], q[lo:lo + c], k[lo:lo + c], v[lo:lo + c]
            xs = []
            for j in range(c):
                diff = jnp.where(row_c >= j, bb - bb[j:j + 1, :], -jnp.inf)
                xs.append(qq * jnp.exp(diff) * kk[j:j + 1, :])
            att = _dot(jnp.concatenate(xs, axis=0).astype(BF16), seg_ref[...])
            o = o_inter[lo:lo + c]
            if blk:
                o = o + o_off[lo - c:lo]
            for j in range(c):
                o = o + att[j * c:(j + 1) * c, :] * vv[j:j + 1, :]
            outs.append(o)
        o = jnp.concatenate(outs, axis=0)

        blast = b[C - 1:C, :]
        khat = (k * jnp.exp(blast - b)).astype(BF16)
        state_ref[...] = ht * jnp.exp(blast) + bd_ref[...] * _dot_tn(v16, khat)
        heads = []
        for h in range(GLA_HEADS):
            hs = slice(h * GLA_DV, (h + 1) * GLA_DV)
            heads.append(_rms(o[:, hs], gn_ref[:, hs]))
        o_ref[rows, :] = jnp.concatenate(heads, axis=1).astype(o_ref.dtype)
        return carry

    lax.fori_loop(0, nchunk, chunk, 0)


def _gla(p_gla, a_up, a_b, g_norm):
    B, S, _ = p_gla.shape
    KW = GLA_HEADS * GLA_DK
    VW = GLA_HEADS * GLA_DV
    tg = _tile(S, TILE_ROWS["gla"])
    assert GLA_CHUNK == GLA_DK and tg % GLA_CHUNK == 0
    aup = jnp.concatenate([a_up, jnp.zeros((GLA_IN_W - 2 * KW - VW - GLA_GATE_RANK, KW), a_up.dtype)], axis=0)
    seg = _block_diag_ones(KW, GLA_DK, GLA_DV, VW, dtype=BF16)
    bdk = _block_diag_ones(KW, GLA_DK, dtype=BF16)
    bd = _block_diag_ones(VW, GLA_DV, GLA_DK, KW)
    full = lambda a: pl.BlockSpec(a.shape, lambda b, t: (0,) * a.ndim)
    return pl.pallas_call(
        functools.partial(_gla_kernel, nchunk=tg // GLA_CHUNK),
        grid=(B, S // tg),
        in_specs=[pl.BlockSpec((None, tg, GLA_IN_W), lambda b, t: (b, t, 0)),
                  full(aup), pl.BlockSpec((1, KW), lambda b, t: (0, 0)), pl.BlockSpec((1, VW), lambda b, t: (0, 0)),
                  full(seg), full(bdk), full(bd)],
        out_specs=pl.BlockSpec((None, tg, VW), lambda b, t: (b, t, 0)),
        out_shape=jax.ShapeDtypeStruct((B, S, VW), BRANCH_DTYPE),
        scratch_shapes=[pltpu.VMEM((VW, KW), F32)],
        compiler_params=pltpu.CompilerParams(dimension_semantics=("parallel", "arbitrary"),
                                             vmem_limit_bytes=VMEM_LIMIT),
        name="gla",
    )(p_gla, aup, a_b, g_norm, seg, bdk, bd)


def _merge_kernel(x_ref, ym_ref, yr_ref, yg_ref, gate_ref, mg_ref, wb_ref, wo_ref, gpost_ref, o_ref):
    W = BRANCH_W
    merged = None
    for n, y_ref in enumerate((ym_ref, yr_ref, yg_ref)):
        ys = (y_ref[...].astype(F32) * jax.nn.silu(gate_ref[:, n * W:(n + 1) * W].astype(F32))).astype(BF16)
        br = _dot(ys, wb_ref[n]) * jax.nn.sigmoid(mg_ref[:, n * D_MODEL:(n + 1) * D_MODEL].astype(F32))
        merged = br if merged is None else merged + br
    out = _dot(merged.astype(BF16), wo_ref[...])
    o_ref[...] = x_ref[...] + _rms(out, gpost_ref[...])


def _merge(x2, ym, yr, yg, gate, mg, wb, wo, gpost):
    N = x2.shape[0]
    W = BRANCH_W
    tm = _tile(N, TILE_ROWS["merge"])
    tile = lambda n: pl.BlockSpec((tm, n), lambda i: (i, 0))
    return pl.pallas_call(
        _merge_kernel,
        grid=(N // tm,),
        in_specs=[tile(D_MODEL), tile(W), tile(W), tile(W), tile(GATE_W), tile(MERGE_W),
                  pl.BlockSpec((N_BRANCH, W, D_MODEL), lambda i: (0, 0, 0)),
                  pl.BlockSpec((D_MODEL, D_MODEL), lambda i: (0, 0)),
                  pl.BlockSpec((1, D_MODEL), lambda i: (0, 0))],
        out_specs=tile(D_MODEL),
        out_shape=jax.ShapeDtypeStruct((N, D_MODEL), F32),
        compiler_params=pltpu.CompilerParams(dimension_semantics=("parallel",), vmem_limit_bytes=VMEM_LIMIT),
        name="merge_out",
    )(x2, ym, yr, yg, gate, mg, wb, wo, gpost)


def kernel(x, positions, norm_pre, w_in, mla_q_norm, mla_kv_norm, mla_w_uq, mla_w_ukv, rwkv_mu, rwkv_w0,
           rwkv_w_up, rwkv_a0, rwkv_a_up, rwkv_k_k, rwkv_k_a, rwkv_r_k, rwkv_ln_w, rwkv_ln_b, gla_a_up, gla_a_b,
           gla_norm, w_branch_out, w_out, norm_post):
    B, S, D = x.shape
    assert D == D_MODEL and S % RWKV_CHUNK == 0
    depth = w_in.shape[0]
    N = B * S
    cosm, sinm = _rope_tables(positions)
    x2 = x.reshape(N, D)
    row = lambda a: a.reshape(1, -1)
    for l in range(depth):
        p_mla, p_rwkv, p_gla, p_gate, p_merge = _inproj(x2, row(norm_pre[l]), _prep_w_in(w_in[l]))
        wq, wqr, wk, wv = _prep_mla_weights(mla_w_uq[l], mla_w_ukv[l])
        q, k, v = _mla_prep(p_mla.reshape(B, S, -1), cosm, sinm, row(mla_q_norm[l]), row(mla_kv_norm[l]),
                            wq, wqr, wk, wv)
        y_mla = _mla_attn(q, k, v)
        y_rwkv = _rwkv(p_rwkv.reshape(B, S, -1), row(rwkv_mu[l]), row(rwkv_w0[l]), rwkv_w_up[l], row(rwkv_a0[l]),
                       rwkv_a_up[l], row(rwkv_k_k[l]), row(rwkv_k_a[l]), rwkv_r_k[l], row(rwkv_ln_w[l]),
                       row(rwkv_ln_b[l]))
        y_gla = _gla(p_gla.reshape(B, S, -1), gla_a_up[l], row(gla_a_b[l]), row(gla_norm[l]))
        x2 = _merge(x2, y_mla.reshape(N, -1), y_rwkv.reshape(N, -1), y_gla.reshape(N, -1), p_gate, p_merge,
                    w_branch_out[l].astype(BF16), w_out[l].astype(BF16), row(norm_post[l]))
    return x2.reshape(B, S, D)
```

```python
import functools

import jax
import jax.numpy as jnp
from jax import lax
from jax.experimental import pallas as pl
from jax.experimental.pallas import tpu as pltpu

F32 = jnp.float32
BF16 = jnp.bfloat16

D_MODEL = 1024
N_BRANCH = 3
BRANCH_W = D_MODEL // 2
NORM_EPS = 1e-6
MLA_HEADS = 8
MLA_NOPE = 64
MLA_ROPE = 32
MLA_V = BRANCH_W // MLA_HEADS
MLA_Q_LORA = 256
MLA_KV_LORA = 128
ROPE_BASE = 10000.0
RWKV_HEADS = 8
RWKV_HEAD = BRANCH_W // RWKV_HEADS
RWKV_DECAY_RANK = 64
RWKV_ICLR_RANK = 64
RWKV_IN = 3 * BRANCH_W + RWKV_DECAY_RANK + RWKV_ICLR_RANK
RWKV_GN_EPS = 64e-5
GLA_HEADS = 4
GLA_DK = 64
GLA_DV = BRANCH_W // GLA_HEADS
GLA_GATE_RANK = 16
GLA_TAU = 16.0
IN_SIZES = (MLA_Q_LORA, MLA_KV_LORA, MLA_ROPE, RWKV_IN,
            GLA_HEADS * GLA_DK, GLA_HEADS * GLA_DK, GLA_HEADS * GLA_DV, GLA_GATE_RANK,
            N_BRANCH * BRANCH_W, N_BRANCH * D_MODEL)

LANE = 128
MLA_IN_W = 640
GLA_IN_W = 1152
GATE_W = N_BRANCH * BRANCH_W
MERGE_W = N_BRANCH * D_MODEL
PROJ_W = MLA_IN_W + RWKV_IN + GLA_IN_W + GATE_W + MERGE_W
QK_W = 128

RWKV_CHUNK = 64
RWKV_QUAD = 256
RWKV_SEQS = 4
GLA_CHUNK = 64
GLA_BLOCK = 16
VMEM_LIMIT = 48 * 1024 * 1024

TILE_ROWS = dict(rope=512, inproj=512, mla_prep=512, attn=512, gla=512, merge=512)


def _tile(n, want):
    t = min(n, want)
    assert n % t == 0, (n, t)
    return t


def _dot(a, b):
    return jnp.dot(a, b, preferred_element_type=F32)


def _dot_nt(a, b):
    return lax.dot_general(a, b, (((1,), (1,)), ((), ())), preferred_element_type=F32)


def _dot_tn(a, b):
    return lax.dot_general(a, b, (((0,), (0,)), ((), ())), preferred_element_type=F32)


def _split(x, parts):
    out = []
    rem = x
    for _ in range(parts):
        t = rem.astype(BF16)
        out.append(t)
        rem = rem - t.astype(F32)
    return out


def _dot_exact_rhs(a, b_bf16, parts=3):
    acc = None
    for t in _split(a, parts):
        d = _dot(t, b_bf16)
        acc = d if acc is None else acc + d
    return acc


def _dot_exact_lhs(a_bf16, b, parts=3):
    acc = None
    for t in _split(b, parts):
        d = _dot(a_bf16, t)
        acc = d if acc is None else acc + d
    return acc


def _dot_hi(a, b):
    a1, a2 = _split(a, 2)
    b1, b2 = _split(b, 2)
    return _dot(a1, b1) + (_dot(a1, b2) + _dot(a2, b1))


def _rms(x, g):
    return x * lax.rsqrt(jnp.mean(x * x, axis=-1, keepdims=True) + NORM_EPS) * g


def _block_diag_ones(n, blk_r, blk_c=None, m=None, dtype=F32):
    blk_c = blk_r if blk_c is None else blk_c
    m = n if m is None else m
    r = jnp.arange(n)[:, None] // blk_r
    c = jnp.arange(m)[None, :] // blk_c
    return (r == c).astype(dtype)


def _rope_kernel(pos_ref, inv_ref, cos_ref, sin_ref):
    ang = pos_ref[...] * inv_ref[...]
    lane = lax.broadcasted_iota(jnp.int32, ang.shape, 1)
    rope = (lane >= MLA_NOPE) & (lane < MLA_NOPE + MLA_ROPE)
    cos_ref[...] = jnp.where(rope, jnp.cos(ang), jnp.where(lane < MLA_NOPE, 1.0, 0.0))
    sin_ref[...] = jnp.where(rope, jnp.sin(ang), 0.0)


def _rope_tables(positions):
    B, S = positions.shape
    ts = _tile(S, TILE_ROWS["rope"])
    inv = 1.0 / (ROPE_BASE ** (jnp.arange(0, MLA_ROPE, 2, dtype=F32) / MLA_ROPE))
    inv_row = jnp.zeros((1, QK_W), F32).at[0, MLA_NOPE:MLA_NOPE + MLA_ROPE].set(jnp.tile(inv, 2))
    pos = positions.astype(F32)[..., None]
    return pl.pallas_call(
        _rope_kernel,
        grid=(B, S // ts),
        in_specs=[pl.BlockSpec((None, ts, 1), lambda b, i: (b, i, 0)),
                  pl.BlockSpec((1, QK_W), lambda b, i: (0, 0))],
        out_specs=[pl.BlockSpec((None, ts, QK_W), lambda b, i: (b, i, 0))] * 2,
        out_shape=[jax.ShapeDtypeStruct((B, S, QK_W), F32)] * 2,
        name="rope_tables",
    )(pos, inv_row)


_PROJ_SPLITS = (MLA_IN_W, RWKV_IN, GLA_IN_W, GATE_W, MERGE_W)
_PROJ_DTYPES = (F32, F32, F32, BF16, BF16)
BRANCH_DTYPE = BF16


def _inproj_kernel(x_ref, g_ref, w_ref, mla_ref, rwkv_ref, gla_ref, gate_ref, merge_ref):
    h = _rms(x_ref[...], g_ref[...]).astype(BF16)
    off = 0
    for ref, n in zip((mla_ref, rwkv_ref, gla_ref, gate_ref, merge_ref), _PROJ_SPLITS):
        ref[...] = _dot(h, w_ref[:, off:off + n]).astype(ref.dtype)
        off += n


def _inproj(x2, g, w):
    N = x2.shape[0]
    tm = _tile(N, TILE_ROWS["inproj"])
    return pl.pallas_call(
        _inproj_kernel,
        grid=(N // tm,),
        in_specs=[pl.BlockSpec((tm, D_MODEL), lambda i: (i, 0)),
                  pl.BlockSpec((1, D_MODEL), lambda i: (0, 0)),
                  pl.BlockSpec((D_MODEL, PROJ_W), lambda i: (0, 0), pipeline_mode=pl.Buffered(1))],
        out_specs=[pl.BlockSpec((tm, n), lambda i: (i, 0)) for n in _PROJ_SPLITS],
        out_shape=[jax.ShapeDtypeStruct((N, n), dt) for n, dt in zip(_PROJ_SPLITS, _PROJ_DTYPES)],
        compiler_params=pltpu.CompilerParams(dimension_semantics=("parallel",), vmem_limit_bytes=VMEM_LIMIT),
        name="inproj",
    )(x2, g, w)


def _prep_w_in(w_in):
    w_in = w_in.astype(BF16)
    offs = [0]
    for n in IN_SIZES:
        offs.append(offs[-1] + n)
    c_q, c_kv, k_rope, u_rwkv, g_q, g_k, g_v, g_lat, br_gate, merge_gate = (
        w_in[:, offs[i]:offs[i + 1]] for i in range(len(IN_SIZES)))
    z = lambda n: jnp.zeros((D_MODEL, n), w_in.dtype)
    half = MLA_ROPE // 2
    k_rot = jnp.concatenate([-k_rope[:, half:], k_rope[:, :half]], axis=1)
    cols = [c_q, c_kv,
            z(MLA_NOPE), k_rope, z(QK_W - MLA_NOPE - MLA_ROPE),
            z(MLA_NOPE), k_rot, z(QK_W - MLA_NOPE - MLA_ROPE),
            u_rwkv,
            g_q, g_k, g_v, g_lat, z(GLA_IN_W - 1024 - GLA_GATE_RANK),
            br_gate, merge_gate]
    return jnp.concatenate(cols, axis=1)


def _mla_prep_kernel(p_ref, cos_ref, sin_ref, qn_ref, kvn_ref, wq_ref, wqr_ref, wk_ref, wv_ref,
                     q_ref, k_ref, v_ref):
    cm = cos_ref[...]
    sm = sin_ref[...]
    scale = (MLA_NOPE + MLA_ROPE) ** -0.5
    cq = _rms(p_ref[:, 0:MLA_Q_LORA], qn_ref[...]).astype(BF16)
    q_all = _dot(cq, wq_ref[...])
    q_rot = _dot(cq, wqr_ref[...])
    ckv = _rms(p_ref[:, MLA_Q_LORA:MLA_Q_LORA + MLA_KV_LORA], kvn_ref[...]).astype(BF16)
    k_all = _dot(ckv, wk_ref[...])
    v_all = _dot(ckv, wv_ref[...])
    o = MLA_Q_LORA + MLA_KV_LORA
    k_r = p_ref[:, o:o + QK_W] * cm + p_ref[:, o + QK_W:o + 2 * QK_W] * sm
    for h in range(MLA_HEADS):
        sl = slice(h * QK_W, (h + 1) * QK_W)
        q_ref[h] = ((q_all[:, sl] * cm + q_rot[:, sl] * sm) * scale).astype(q_ref.dtype)
        k_ref[h] = (k_all[:, sl] + k_r).astype(k_ref.dtype)
    for hp in range(MLA_HEADS // 2):
        v_ref[hp] = v_all[:, hp * LANE:(hp + 1) * LANE].astype(v_ref.dtype)


def _prep_mla_weights(w_uq, w_ukv):
    H = MLA_HEADS
    half = MLA_ROPE // 2
    wq = w_uq.reshape(MLA_Q_LORA, H, MLA_NOPE + MLA_ROPE)
    nope, rope = wq[..., :MLA_NOPE], wq[..., MLA_NOPE:]
    rot = jnp.concatenate([-rope[..., half:], rope[..., :half]], axis=-1)
    zq = lambda n: jnp.zeros((MLA_Q_LORA, H, n), w_uq.dtype)
    pad = QK_W - MLA_NOPE - MLA_ROPE
    wq_main = jnp.concatenate([nope, rope, zq(pad)], axis=-1).reshape(MLA_Q_LORA, H * QK_W)
    wq_rot = jnp.concatenate([zq(MLA_NOPE), rot, zq(pad)], axis=-1).reshape(MLA_Q_LORA, H * QK_W)
    wkv = w_ukv.reshape(MLA_KV_LORA, H, MLA_NOPE + MLA_V)
    wk = jnp.concatenate([wkv[..., :MLA_NOPE], jnp.zeros((MLA_KV_LORA, H, QK_W - MLA_NOPE), w_ukv.dtype)],
                         axis=-1).reshape(MLA_KV_LORA, H * QK_W)
    wv = wkv[..., MLA_NOPE:].reshape(MLA_KV_LORA, H * MLA_V)
    return wq_main.astype(BF16), wq_rot.astype(BF16), wk.astype(BF16), wv.astype(BF16)


def _mla_prep(p_mla, cosm, sinm, q_norm, kv_norm, wq, wqr, wk, wv):
    B, S, _ = p_mla.shape
    H = MLA_HEADS
    ts = _tile(S, TILE_ROWS["mla_prep"])
    const = lambda shape: pl.BlockSpec(shape, lambda b, i: (0,) * len(shape))
    return pl.pallas_call(
        _mla_prep_kernel,
        grid=(B, S // ts),
        in_specs=[pl.BlockSpec((None, ts, MLA_IN_W), lambda b, i: (b, i, 0)),
                  pl.BlockSpec((None, ts, QK_W), lambda b, i: (b, i, 0)),
                  pl.BlockSpec((None, ts, QK_W), lambda b, i: (b, i, 0)),
                  const((1, MLA_Q_LORA)), const((1, MLA_KV_LORA)),
                  const(wq.shape), const(wqr.shape), const(wk.shape), const(wv.shape)],
        out_specs=[pl.BlockSpec((None, H, ts, QK_W), lambda b, i: (b, 0, i, 0)),
                   pl.BlockSpec((None, H, ts, QK_W), lambda b, i: (b, 0, i, 0)),
                   pl.BlockSpec((None, H // 2, ts, LANE), lambda b, i: (b, 0, i, 0))],
        out_shape=[jax.ShapeDtypeStruct((B, H, S, QK_W), BF16),
                   jax.ShapeDtypeStruct((B, H, S, QK_W), BF16),
                   jax.ShapeDtypeStruct((B, H // 2, S, LANE), BF16)],
        compiler_params=pltpu.CompilerParams(dimension_semantics=("parallel", "parallel"),
                                             vmem_limit_bytes=VMEM_LIMIT),
        name="mla_prep",
    )(p_mla, cosm, sinm, q_norm, kv_norm, wq, wqr, wk, wv)


def _attn_kernel(q_ref, k_ref, v_ref, o_ref, acc_ref, s_ref, *, tq):
    qi = pl.program_id(2)
    lane = lax.broadcasted_iota(jnp.int32, (1, LANE), 1)
    first = lane < MLA_V
    acc_ref[...] = jnp.zeros_like(acc_ref)

    def scores(j, slot):
        rows = pl.ds(pl.multiple_of(j * tq, tq), tq)
        for e in range(2):
            s_ref[slot, e] = _dot_nt(q_ref[e], k_ref[e, rows, :])

    def consume(j, slot, carry, masked):
        m0, l0, m1, l1 = carry
        rows = pl.ds(pl.multiple_of(j * tq, tq), tq)
        v = v_ref[rows, :]
        zero = jnp.zeros_like(v)
        outs = []
        for e, m, l in ((0, m0, l0), (1, m1, l1)):
            s = s_ref[slot, e]
            if masked:
                r = lax.broadcasted_iota(jnp.int32, s.shape, 0)
                c = lax.broadcasted_iota(jnp.int32, s.shape, 1)
                s = jnp.where(c <= r, s, -jnp.inf)
            m_new = jnp.maximum(m, jnp.max(s, axis=-1, keepdims=True))
            p = jnp.exp(s - m_new)
            alpha = jnp.exp(m - m_new)
            l_new = alpha * l + jnp.sum(p, axis=-1, keepdims=True)
            ve = jnp.where(first, v, zero) if e == 0 else jnp.where(first, zero, v)
            outs.append((m_new, l_new, alpha, _dot(p.astype(v.dtype), ve)))
        (m0, l0, a0, pv0), (m1, l1, a1, pv1) = outs
        acc_ref[...] = acc_ref[...] * jnp.where(first, a0, a1) + (pv0 + pv1)
        return m0, l0, m1, l1

    def pair(i, carry):
        scores(2 * i + 1, 1)
        carry = consume(2 * i, 0, carry, False)
        scores(2 * i + 2, 0)
        return consume(2 * i + 1, 1, carry, False)

    def even_tail(carry):
        return consume(qi, 0, carry, True)

    def odd_tail(carry):
        scores(qi, 1)
        return consume(qi, 1, consume(qi - 1, 0, carry, False), True)

    neg = jnp.full((tq, 1), -jnp.inf, F32)
    zer = jnp.zeros((tq, 1), F32)
    scores(0, 0)
    carry = lax.fori_loop(0, qi // 2, pair, (neg, zer, neg, zer))
    _, l0, _, l1 = lax.cond(qi % 2 == 0, even_tail, odd_tail, carry)
    o_ref[...] = (acc_ref[...] / jnp.where(first, l0, l1)).astype(o_ref.dtype)


def _mla_attn(q, k, v):
    B, H, S, _ = q.shape
    tq = _tile(S, TILE_ROWS["attn"])
    return pl.pallas_call(
        functools.partial(_attn_kernel, tq=tq),
        grid=(B, H // 2, S // tq),
        in_specs=[pl.BlockSpec((None, 2, tq, QK_W), lambda b, h, i: (b, h, i, 0)),
                  pl.BlockSpec((None, 2, S, QK_W), lambda b, h, i: (b, h, 0, 0)),
                  pl.BlockSpec((None, None, S, LANE), lambda b, h, i: (b, h, 0, 0))],
        out_specs=pl.BlockSpec((None, tq, LANE), lambda b, h, i: (b, i, h)),
        out_shape=jax.ShapeDtypeStruct((B, S, BRANCH_W), BRANCH_DTYPE),
        scratch_shapes=[pltpu.VMEM((tq, LANE), F32), pltpu.VMEM((2, 2, tq, tq), F32)],
        compiler_params=pltpu.CompilerParams(dimension_semantics=("parallel", "parallel", "arbitrary"),
                                             vmem_limit_bytes=VMEM_LIMIT),
        name="mla_attn",
    )(q, k, v)


def _expand(x, bd16):
    x16 = x.astype(BF16)
    C, G = x.shape
    zero = jnp.zeros((C, LANE), BF16)
    blocks = []
    for h in range(bd16.shape[0] // C):
        lo = (h * RWKV_HEAD // LANE) * LANE
        own = x16[:, lo:lo + LANE] * bd16[h * C:(h + 1) * C, lo:lo + LANE]
        blocks.append(jnp.concatenate([own if l0 == lo else zero for l0 in range(0, G, LANE)], axis=1))
    return jnp.concatenate(blocks, axis=0)


def _tri_inverse(ns, eye, bd16):
    C = ns[0].shape[0]
    ps = [eye + n for n in ns]
    ss = [_dot(n.astype(BF16), _expand(n, bd16)) for n in ns]
    for _ in range(C.bit_length() - 3):
        sps = [_dot(jnp.concatenate([s, p], axis=0).astype(BF16), _expand(s, bd16)) for s, p in zip(ss, ps)]
        ss = [sp[:C] for sp in sps]
        ps = [p + sp[C:] for p, sp in zip(ps, sps)]
    return [p + _dot(p.astype(BF16), _expand(s, bd16)) for s, p in zip(ss, ps)]


def _rwkv_kernel(u_ref, mu_ref, w0_ref, wup_ref, a0_ref, aup_ref, kk_ref, ka_ref, rk_ref, lnw_ref, lnb_ref,
                 bd_ref, bd16_ref, o_ref, state_ref, last_ref):
    C = RWKV_CHUNK
    W = BRANCH_W
    G = RWKV_QUAD
    NB = u_ref.shape[0]
    R = NB * C
    quads = [slice(q * G, (q + 1) * G) for q in range(W // G)]
    seqs = [slice(e * C, (e + 1) * C) for e in range(NB)]

    @pl.when(pl.program_id(1) == 0)
    def _():
        state_ref[...] = jnp.zeros_like(state_ref)
        last_ref[...] = jnp.zeros_like(last_ref)

    u = u_ref[...].reshape(R, RWKV_IN)
    row = lax.broadcasted_iota(jnp.int32, (R, 1), 0)
    prev = pltpu.roll(u, 1, axis=0)
    for e in range(NB):
        prev = jnp.where(row == e * C, last_ref[e], prev)
        last_ref[e] = u[(e + 1) * C - 1:(e + 1) * C, :]
    u = u + (prev - u) * mu_ref[...]
    r, k, v = u[:, 0:W], u[:, W:2 * W], u[:, 2 * W:3 * W]
    lwa = u[:, 3 * W:]
    bd16 = bd16_ref[...]

    def segsum(x):
        return jnp.concatenate([_dot_exact_rhs(x[:, sl], bd16, parts=2) for sl in quads], axis=1)

    w = -jax.nn.softplus(-(w0_ref[...] + _dot_hi(jnp.tanh(lwa), wup_ref[...]))) - 0.5
    logd = -jnp.exp(w)
    a = jax.nn.sigmoid(a0_ref[...] + _dot_hi(lwa, aup_ref[...]))
    kk = k * kk_ref[...]
    kk = kk / jnp.maximum(jnp.sqrt(segsum(kk * kk)), 1e-12)
    k = k * (1.0 + (a - 1.0) * ka_ref[...])
    bonus = segsum(r * k * rk_ref[...]) * v

    tr = lax.broadcasted_iota(jnp.int32, (R, R), 0)
    tc = lax.broadcasted_iota(jnp.int32, (R, R), 1)
    tri = jnp.where((tc <= tr) & (tc >= (tr // C) * C), 1.0, 0.0).astype(BF16)
    b = _dot_exact_lhs(tri, logd)
    btot = jnp.concatenate([jnp.broadcast_to(b[sq.stop - 1:sq.stop, :], (C, W)) for sq in seqs], axis=0)
    enb = jnp.exp(-b)
    rt = r * jnp.exp(b)
    alt = -kk * jnp.exp(b - logd)
    beta = kk * a
    bt = beta * enb
    kt = k * enb
    edec = jnp.exp(btot - b)
    bhat = beta * edec
    khat = k * edec
    gams = [jnp.exp(b[sq.stop - 1:sq.stop, :]) for sq in seqs]

    t_idx = lax.broadcasted_iota(jnp.int32, (C, G), 0)
    s_idx = lax.broadcasted_iota(jnp.int32, (C, G), 1) % C
    incl = s_idx <= t_idx
    strict = s_idx < t_idx
    eye = jnp.where(s_idx == t_idx, 1.0, 0.0)

    chains = [(e, q, sq, sl) for e, sq in enumerate(seqs) for q, sl in enumerate(quads)]
    ars = [jnp.concatenate([alt[sq, sl], rt[sq, sl]], axis=0).astype(BF16) for _, _, sq, sl in chains]
    abs_ = [_dot_nt(ar, _expand(bt[sq, sl], bd16)) for ar, (_, _, sq, sl) in zip(ars, chains)]
    aks = [_dot_nt(ar, _expand(kt[sq, sl], bd16)) for ar, (_, _, sq, sl) in zip(ars, chains)]
    t16s = [t.astype(BF16) for t in _tri_inverse([jnp.where(strict, ab[:C], 0.0) for ab in abs_], eye, bd16)]
    m16s = [_dot(t16, _expand(jnp.where(strict, ak[:C], 0.0), bd16)).astype(BF16) for t16, ak in zip(t16s, aks)]
    a_rbs = [jnp.where(incl, ab[C:], 0.0).astype(BF16) for ab in abs_]
    a_rks = [jnp.where(incl, ak[C:], 0.0).astype(BF16) for ak in aks]
    evs = [_expand(v[sq, sl], bd16) for _, _, sq, sl in chains]
    hts = [state_ref[e, q] for e, q, _, _ in chains]
    prs = [_dot_nt(ar, ht.astype(BF16)) for ar, ht in zip(ars, hts)]
    us = [_dot(t16, _expand(pr[:C], bd16)) + _dot(m16, ev) for t16, pr, m16, ev in zip(t16s, prs, m16s, evs)]
    for (e, q, sq, sl), ht, uq in zip(chains, hts, us):
        z = jnp.concatenate([v[sq, sl], uq], axis=0).astype(BF16)
        wk = jnp.concatenate([khat[sq, sl], bhat[sq, sl]], axis=0).astype(BF16)
        state_ref[e, q] = ht * gams[e][:, sl] + bd_ref[...] * _dot_tn(z, wk)
    ys = [pr[C:] + _dot(a_rb, _expand(uq, bd16)) + _dot(a_rk, ev)
          for pr, a_rb, uq, a_rk, ev in zip(prs, a_rbs, us, a_rks, evs)]
    nq = len(quads)
    y = jnp.concatenate([jnp.concatenate(ys[e * nq:(e + 1) * nq], axis=1) for e in range(NB)], axis=0)
    mean = segsum(y) * (1.0 / RWKV_HEAD)
    d = y - mean
    var = segsum(d * d) * (1.0 / RWKV_HEAD)
    y = d * lax.rsqrt(var + RWKV_GN_EPS)
    o_ref[...] = (y * lnw_ref[...] + lnb_ref[...] + bonus).reshape(NB, C, W).astype(o_ref.dtype)


def _rwkv(p_rwkv, mu, w0, w_up, a0, a_up, k_k, k_a, r_k, ln_w, ln_b):
    B, S, _ = p_rwkv.shape
    C, W, G = RWKV_CHUNK, BRANCH_W, RWKV_QUAD
    NB = _tile(B, RWKV_SEQS)
    assert C == RWKV_HEAD
    wup = jnp.concatenate([w_up, jnp.zeros_like(a_up)], axis=0)
    aup = jnp.concatenate([jnp.zeros_like(w_up), a_up], axis=0)
    bd = _block_diag_ones(G, RWKV_HEAD)
    row = lambda n: pl.BlockSpec((1, n), lambda b, c: (0, 0))
    full = lambda a: pl.BlockSpec(a.shape, lambda b, c: (0,) * a.ndim)
    return pl.pallas_call(
        _rwkv_kernel,
        grid=(B // NB, S // C),
        in_specs=[pl.BlockSpec((NB, C, RWKV_IN), lambda b, c: (b, c, 0)),
                  row(RWKV_IN), row(W), full(wup), row(W), full(aup), row(W), row(W), row(W), row(W), row(W),
                  full(bd), full(bd)],
        out_specs=pl.BlockSpec((NB, C, W), lambda b, c: (b, c, 0)),
        out_shape=jax.ShapeDtypeStruct((B, S, W), BRANCH_DTYPE),
        scratch_shapes=[pltpu.VMEM((NB, W // G, G, G), F32), pltpu.VMEM((NB, 1, RWKV_IN), F32)],
        compiler_params=pltpu.CompilerParams(dimension_semantics=("parallel", "arbitrary"),
                                             vmem_limit_bytes=VMEM_LIMIT),
        name="rwkv7",
    )(p_rwkv, mu, w0, wup, a0, aup, k_k, k_a, r_k.reshape(1, W), ln_w, ln_b, bd, bd.astype(BF16))


def _gla_kernel(p_ref, aup_ref, ab_ref, gn_ref, seg_ref, bdk_ref, bd_ref, o_ref, state_ref, *, nchunk):
    C = GLA_CHUNK
    c = GLA_BLOCK
    nb = C // c
    KW = GLA_HEADS * GLA_DK
    VW = GLA_HEADS * GLA_DV
    reps = GLA_HEADS

    @pl.when(pl.program_id(1) == 0)
    def _():
        state_ref[...] = jnp.zeros_like(state_ref)

    tr = lax.broadcasted_iota(jnp.int32, (C, C), 0)
    tc = lax.broadcasted_iota(jnp.int32, (C, C), 1)
    tri = jnp.where(tc <= tr, 1.0, 0.0).astype(BF16)
    row_c = lax.broadcasted_iota(jnp.int32, (c, 1), 0)
    row_C = lax.broadcasted_iota(jnp.int32, (C, 1), 0)

    def chunk(i, carry):
        rows = pl.ds(pl.multiple_of(i * C, C), C)
        q = p_ref[rows, 0:KW] * (GLA_DK ** -0.5)
        k = p_ref[rows, KW:2 * KW]
        v = p_ref[rows, 2 * KW:2 * KW + VW]
        lat = p_ref[rows, 2 * KW + VW:]
        z = _dot_hi(lat, aup_ref[...]) + ab_ref[...]
        log_a = jax.nn.log_sigmoid(z) * (1.0 / GLA_TAU)
        b = _dot_exact_lhs(tri, log_a)
        ht = state_ref[...]
        o_inter = _dot_nt((q * jnp.exp(b)).astype(BF16), ht.astype(BF16))

        v16 = v.astype(BF16)
        ev = jnp.concatenate([v16] * reps, axis=0) * seg_ref[...]
        atts = []
        for blk in range(1, nb):
            lo = blk * c
            m = b[lo - 1:lo, :]
            qb = q[lo:lo + c] * jnp.exp(b[lo:lo + c] - m)
            kb = (k * jnp.exp(jnp.where(row_C < lo, m - b, -jnp.inf))).astype(BF16)
            ke = jnp.concatenate([kb] * reps, axis=0) * bdk_ref[...]
            atts.append(_dot_nt(qb.astype(BF16), ke))
        o_off = _dot(jnp.concatenate(atts, axis=0).astype(BF16), ev)

        outs = []
        for blk in range(nb):
            lo = blk * c
            bb, qq, kk, vv = b[lo:lo + c], q[lo:lo + c], k[lo:lo + c], v[lo:lo + c]
            xs = []
            for j in range(c):
                diff = jnp.where(row_c >= j, bb - bb[j:j + 1, :], -jnp.inf)
                xs.append(qq * jnp.exp(diff) * kk[j:j + 1, :])
            att = _dot(jnp.concatenate(xs, axis=0).astype(BF16), seg_ref[...])
            o = o_inter[lo:lo + c]
            if blk:
                o = o + o_off[lo - c:lo]
            for j in range(c):
                o = o + att[j * c:(j + 1) * c, :] * vv[j:j + 1, :]
            outs.append(o)
        o = jnp.concatenate(outs, axis=0)

        blast = b[C - 1:C, :]
        khat = (k * jnp.exp(blast - b)).astype(BF16)
        state_ref[...] = ht * jnp.exp(blast) + bd_ref[...] * _dot_tn(v16, khat)
        heads = []
        for h in range(GLA_HEADS):
            hs = slice(h * GLA_DV, (h + 1) * GLA_DV)
            heads.append(_rms(o[:, hs], gn_ref[:, hs]))
        o_ref[rows, :] = jnp.concatenate(heads, axis=1).astype(o_ref.dtype)
        return carry

    lax.fori_loop(0, nchunk, chunk, 0)


def _gla(p_gla, a_up, a_b, g_norm):
    B, S, _ = p_gla.shape
    KW = GLA_HEADS * GLA_DK
    VW = GLA_HEADS * GLA_DV
    tg = _tile(S, TILE_ROWS["gla"])
    assert GLA_CHUNK == GLA_DK and tg % GLA_CHUNK == 0
    aup = jnp.concatenate([a_up, jnp.zeros((GLA_IN_W - 2 * KW - VW - GLA_GATE_RANK, KW), a_up.dtype)], axis=0)
    seg = _block_diag_ones(KW, GLA_DK, GLA_DV, VW, dtype=BF16)
    bdk = _block_diag_ones(KW, GLA_DK, dtype=BF16)
    bd = _block_diag_ones(VW, GLA_DV, GLA_DK, KW)
    full = lambda a: pl.BlockSpec(a.shape, lambda b, t: (0,) * a.ndim)
    return pl.pallas_call(
        functools.partial(_gla_kernel, nchunk=tg // GLA_CHUNK),
        grid=(B, S // tg),
        in_specs=[pl.BlockSpec((None, tg, GLA_IN_W), lambda b, t: (b, t, 0)),
                  full(aup), pl.BlockSpec((1, KW), lambda b, t: (0, 0)), pl.BlockSpec((1, VW), lambda b, t: (0, 0)),
                  full(seg), full(bdk), full(bd)],
        out_specs=pl.BlockSpec((None, tg, VW), lambda b, t: (b, t, 0)),
        out_shape=jax.ShapeDtypeStruct((B, S, VW), BRANCH_DTYPE),
        scratch_shapes=[pltpu.VMEM((VW, KW), F32)],
        compiler_params=pltpu.CompilerParams(dimension_semantics=("parallel", "arbitrary"),
                                             vmem_limit_bytes=VMEM_LIMIT),
        name="gla",
    )(p_gla, aup, a_b, g_norm, seg, bdk, bd)


def _merge_kernel(x_ref, ym_ref, yr_ref, yg_ref, gate_ref, mg_ref, wb_ref, wo_ref, gpost_ref, o_ref):
    W = BRANCH_W
    merged = None
    for n, y_ref in enumerate((ym_ref, yr_ref, yg_ref)):
        ys = (y_ref[...].astype(F32) * jax.nn.silu(gate_ref[:, n * W:(n + 1) * W].astype(F32))).astype(BF16)
        br = _dot(ys, wb_ref[n]) * jax.nn.sigmoid(mg_ref[:, n * D_MODEL:(n + 1) * D_MODEL].astype(F32))
        merged = br if merged is None else merged + br
    out = _dot(merged.astype(BF16), wo_ref[...])
    o_ref[...] = x_ref[...] + _rms(out, gpost_ref[...])


def _merge(x2, ym, yr, yg, gate, mg, wb, wo, gpost):
    N = x2.shape[0]
    W = BRANCH_W
    tm = _tile(N, TILE_ROWS["merge"])
    tile = lambda n: pl.BlockSpec((tm, n), lambda i: (i, 0))
    return pl.pallas_call(
        _merge_kernel,
        grid=(N // tm,),
        in_specs=[tile(D_MODEL), tile(W), tile(W), tile(W), tile(GATE_W), tile(MERGE_W),
                  pl.BlockSpec((N_BRANCH, W, D_MODEL), lambda i: (0, 0, 0)),
                  pl.BlockSpec((D_MODEL, D_MODEL), lambda i: (0, 0)),
                  pl.BlockSpec((1, D_MODEL), lambda i: (0, 0))],
        out_specs=tile(D_MODEL),
        out_shape=jax.ShapeDtypeStruct((N, D_MODEL), F32),
        compiler_params=pltpu.CompilerParams(dimension_semantics=("parallel",), vmem_limit_bytes=VMEM_LIMIT),
        name="merge_out",
    )(x2, ym, yr, yg, gate, mg, wb, wo, gpost)


def kernel(x, positions, norm_pre, w_in, mla_q_norm, mla_kv_norm, mla_w_uq, mla_w_ukv, rwkv_mu, rwkv_w0,
           rwkv_w_up, rwkv_a0, rwkv_a_up, rwkv_k_k, rwkv_k_a, rwkv_r_k, rwkv_ln_w, rwkv_ln_b, gla_a_up, gla_a_b,
           gla_norm, w_branch_out, w_out, norm_post):
    B, S, D = x.shape
    assert D == D_MODEL and S % RWKV_CHUNK == 0
    depth = w_in.shape[0]
    N = B * S
    cosm, sinm = _rope_tables(positions)
    x2 = x.reshape(N, D)
    row = lambda a: a.reshape(1, -1)
    for l in range(depth):
        p_mla, p_rwkv, p_gla, p_gate, p_merge = _inproj(x2, row(norm_pre[l]), _prep_w_in(w_in[l]))
        wq, wqr, wk, wv = _prep_mla_weights(mla_w_uq[l], mla_w_ukv[l])
        q, k, v = _mla_prep(p_mla.reshape(B, S, -1), cosm, sinm, row(mla_q_norm[l]), row(mla_kv_norm[l]),
                            wq, wqr, wk, wv)
        y_mla = _mla_attn(q, k, v)
        y_rwkv = _rwkv(p_rwkv.reshape(B, S, -1), row(rwkv_mu[l]), row(rwkv_w0[l]), rwkv_w_up[l], row(rwkv_a0[l]),
                       rwkv_a_up[l], row(rwkv_k_k[l]), row(rwkv_k_a[l]), rwkv_r_k[l], row(rwkv_ln_w[l]),
                       row(rwkv_ln_b[l]))
        y_gla = _gla(p_gla.reshape(B, S, -1), gla_a_up[l], row(gla_a_b[l]), row(gla_norm[l]))
        x2 = _merge(x2, y_mla.reshape(N, -1), y_rwkv.reshape(N, -1), y_gla.reshape(N, -1), p_gate, p_merge,
                    w_branch_out[l].astype(BF16), w_out[l].astype(BF16), row(norm_post[l]))
    return x2.reshape(B, S, D)
```

```python
import functools

import jax
import jax.numpy as jnp
from jax import lax
from jax.experimental import pallas as pl
from jax.experimental.pallas import tpu as pltpu

F32 = jnp.float32
BF16 = jnp.bfloat16

D_MODEL = 1024
N_BRANCH = 3
BRANCH_W = D_MODEL // 2
NORM_EPS = 1e-6
MLA_HEADS = 8
MLA_NOPE = 64
MLA_ROPE = 32
MLA_V = BRANCH_W // MLA_HEADS
MLA_Q_LORA = 256
MLA_KV_LORA = 128
ROPE_BASE = 10000.0
RWKV_HEADS = 8
RWKV_HEAD = BRANCH_W // RWKV_HEADS
RWKV_DECAY_RANK = 64
RWKV_ICLR_RANK = 64
RWKV_IN = 3 * BRANCH_W + RWKV_DECAY_RANK + RWKV_ICLR_RANK
RWKV_GN_EPS = 64e-5
GLA_HEADS = 4
GLA_DK = 64
GLA_DV = BRANCH_W // GLA_HEADS
GLA_GATE_RANK = 16
GLA_TAU = 16.0
IN_SIZES = (MLA_Q_LORA, MLA_KV_LORA, MLA_ROPE, RWKV_IN,
            GLA_HEADS * GLA_DK, GLA_HEADS * GLA_DK, GLA_HEADS * GLA_DV, GLA_GATE_RANK,
            N_BRANCH * BRANCH_W, N_BRANCH * D_MODEL)

LANE = 128
MLA_IN_W = 640
GLA_IN_W = 1152
GATE_W = N_BRANCH * BRANCH_W
MERGE_W = N_BRANCH * D_MODEL
PROJ_W = MLA_IN_W + RWKV_IN + GLA_IN_W + GATE_W + MERGE_W
QK_W = 128

RWKV_CHUNK = 64
RWKV_QUAD = 256
RWKV_SEQS = 4
GLA_CHUNK = 64
GLA_BLOCK = 16
VMEM_LIMIT = 56 * 1024 * 1024

TILE_ROWS = dict(rope=512, inproj=512, attn=512, gla=256, merge=512)


def _tile(n, want):
    t = min(n, want)
    assert n % t == 0, (n, t)
    return t


def _dot(a, b):
    return jnp.dot(a, b, preferred_element_type=F32)


def _dot_nt(a, b):
    return lax.dot_general(a, b, (((1,), (1,)), ((), ())), preferred_element_type=F32)


def _dot_tn(a, b):
    return lax.dot_general(a, b, (((0,), (0,)), ((), ())), preferred_element_type=F32)


def _split(x, parts):
    out = []
    rem = x
    for _ in range(parts):
        t = rem.astype(BF16)
        out.append(t)
        rem = rem - t.astype(F32)
    return out


def _dot_exact_rhs(a, b_bf16, parts=3):
    acc = None
    for t in _split(a, parts):
        d = _dot(t, b_bf16)
        acc = d if acc is None else acc + d
    return acc


def _dot_exact_lhs(a_bf16, b, parts=3):
    acc = None
    for t in _split(b, parts):
        d = _dot(a_bf16, t)
        acc = d if acc is None else acc + d
    return acc


def _dot_hi(a, b):
    a1, a2 = _split(a, 2)
    b1, b2 = _split(b, 2)
    return _dot(a1, b1) + (_dot(a1, b2) + _dot(a2, b1))


def _rms(x, g):
    return x * lax.rsqrt(jnp.mean(x * x, axis=-1, keepdims=True) + NORM_EPS) * g


def _block_diag_ones(n, blk_r, blk_c=None, m=None, dtype=F32):
    blk_c = blk_r if blk_c is None else blk_c
    m = n if m is None else m
    r = jnp.arange(n)[:, None] // blk_r
    c = jnp.arange(m)[None, :] // blk_c
    return (r == c).astype(dtype)


def _rope_kernel(pos_ref, inv_ref, cos_ref, sin_ref):
    ang = pos_ref[...] * inv_ref[...]
    lane = lax.broadcasted_iota(jnp.int32, ang.shape, 1)
    rope = (lane >= MLA_NOPE) & (lane < MLA_NOPE + MLA_ROPE)
    cos_ref[...] = jnp.where(rope, jnp.cos(ang), jnp.where(lane < MLA_NOPE, 1.0, 0.0))
    sin_ref[...] = jnp.where(rope, jnp.sin(ang), 0.0)


def _rope_tables(positions):
    B, S = positions.shape
    ts = _tile(S, TILE_ROWS["rope"])
    inv = 1.0 / (ROPE_BASE ** (jnp.arange(0, MLA_ROPE, 2, dtype=F32) / MLA_ROPE))
    inv_row = jnp.zeros((1, QK_W), F32).at[0, MLA_NOPE:MLA_NOPE + MLA_ROPE].set(jnp.tile(inv, 2))
    pos = positions.astype(F32)[..., None]
    return pl.pallas_call(
        _rope_kernel,
        grid=(B, S // ts),
        in_specs=[pl.BlockSpec((None, ts, 1), lambda b, i: (b, i, 0)),
                  pl.BlockSpec((1, QK_W), lambda b, i: (0, 0))],
        out_specs=[pl.BlockSpec((None, ts, QK_W), lambda b, i: (b, i, 0))] * 2,
        out_shape=[jax.ShapeDtypeStruct((B, S, QK_W), F32)] * 2,
        name="rope_tables",
    )(pos, inv_row)


_PROJ_SPLITS = (MLA_IN_W, RWKV_IN, GLA_IN_W, GATE_W, MERGE_W)
_PROJ_DTYPES = (F32, F32, F32, BF16, BF16)
BRANCH_DTYPE = BF16


def _inproj_kernel(x_ref, g_ref, w_ref, cos_ref, sin_ref, qn_ref, kvn_ref, wq_ref, wqr_ref, wk_ref, wv_ref,
                   q_ref, k_ref, v_ref, rwkv_ref, gla_ref, gate_ref, merge_ref):
    h = _rms(x_ref[...], g_ref[...]).astype(BF16)
    _mla_heads(_dot(h, w_ref[:, 0:MLA_IN_W]), cos_ref[...], sin_ref[...], qn_ref, kvn_ref, wq_ref, wqr_ref, wk_ref,
               wv_ref, q_ref, k_ref, v_ref)
    off = MLA_IN_W
    for ref, n in zip((rwkv_ref, gla_ref, gate_ref, merge_ref), _PROJ_SPLITS[1:]):
        ref[...] = _dot(h, w_ref[:, off:off + n]).astype(ref.dtype)
        off += n


def _inproj(x2, g, w, cosm, sinm, q_norm, kv_norm, wq, wqr, wk, wv, B, S):
    N = x2.shape[0]
    H = MLA_HEADS
    tm = _tile(S, TILE_ROWS["inproj"])
    per_seq = S // tm
    const = lambda a: pl.BlockSpec(a.shape, lambda i: (0,) * a.ndim)
    seq_tile = lambda i: (i // per_seq, i % per_seq, 0)
    head_tile = lambda i: (i // per_seq, 0, i % per_seq, 0)
    return pl.pallas_call(
        _inproj_kernel,
        grid=(N // tm,),
        in_specs=[pl.BlockSpec((tm, D_MODEL), lambda i: (i, 0)),
                  pl.BlockSpec((1, D_MODEL), lambda i: (0, 0)),
                  pl.BlockSpec((D_MODEL, PROJ_W), lambda i: (0, 0), pipeline_mode=pl.Buffered(1)),
                  pl.BlockSpec((None, tm, QK_W), seq_tile), pl.BlockSpec((None, tm, QK_W), seq_tile),
                  const(q_norm), const(kv_norm), const(wq), const(wqr), const(wk), const(wv)],
        out_specs=[pl.BlockSpec((None, H, tm, QK_W), head_tile), pl.BlockSpec((None, H, tm, QK_W), head_tile),
                   pl.BlockSpec((None, H // 2, tm, LANE), head_tile)]
                  + [pl.BlockSpec((tm, n), lambda i: (i, 0)) for n in _PROJ_SPLITS[1:]],
        out_shape=[jax.ShapeDtypeStruct((B, H, S, QK_W), BF16), jax.ShapeDtypeStruct((B, H, S, QK_W), BF16),
                   jax.ShapeDtypeStruct((B, H // 2, S, LANE), BF16)]
                  + [jax.ShapeDtypeStruct((N, n), dt) for n, dt in zip(_PROJ_SPLITS[1:], _PROJ_DTYPES[1:])],
        compiler_params=pltpu.CompilerParams(dimension_semantics=("parallel",), vmem_limit_bytes=VMEM_LIMIT),
        name="inproj",
    )(x2, g, w, cosm, sinm, q_norm, kv_norm, wq, wqr, wk, wv)


def _prep_w_in(w_in):
    offs = [0]
    for n in IN_SIZES:
        offs.append(offs[-1] + n)
    c_q, c_kv, k_rope, u_rwkv, g_q, g_k, g_v, g_lat, br_gate, merge_gate = (
        w_in[:, offs[i]:offs[i + 1]] for i in range(len(IN_SIZES)))
    z = lambda n: jnp.zeros((D_MODEL, n), w_in.dtype)
    half = MLA_ROPE // 2
    k_rot = jnp.concatenate([-k_rope[:, half:], k_rope[:, :half]], axis=1)
    cols = [c_q, c_kv,
            z(MLA_NOPE), k_rope, z(QK_W - MLA_NOPE - MLA_ROPE),
            z(MLA_NOPE), k_rot, z(QK_W - MLA_NOPE - MLA_ROPE),
            u_rwkv,
            g_q, g_k, g_v, g_lat, z(GLA_IN_W - 1024 - GLA_GATE_RANK),
            br_gate, merge_gate]
    return jnp.concatenate(cols, axis=1).astype(BF16)


def _mla_heads(p, cm, sm, qn_ref, kvn_ref, wq_ref, wqr_ref, wk_ref, wv_ref, q_ref, k_ref, v_ref):
    scale = (MLA_NOPE + MLA_ROPE) ** -0.5
    cq = _rms(p[:, 0:MLA_Q_LORA], qn_ref[...]).astype(BF16)
    q_all = _dot(cq, wq_ref[...])
    q_rot = _dot(cq, wqr_ref[...])
    ckv = _rms(p[:, MLA_Q_LORA:MLA_Q_LORA + MLA_KV_LORA], kvn_ref[...]).astype(BF16)
    k_all = _dot(ckv, wk_ref[...])
    v_all = _dot(ckv, wv_ref[...])
    o = MLA_Q_LORA + MLA_KV_LORA
    k_r = p[:, o:o + QK_W] * cm + p[:, o + QK_W:o + 2 * QK_W] * sm
    for h in range(MLA_HEADS):
        sl = slice(h * QK_W, (h + 1) * QK_W)
        q_ref[h] = ((q_all[:, sl] * cm + q_rot[:, sl] * sm) * scale).astype(q_ref.dtype)
        k_ref[h] = (k_all[:, sl] + k_r).astype(k_ref.dtype)
    for hp in range(MLA_HEADS // 2):
        v_ref[hp] = v_all[:, hp * LANE:(hp + 1) * LANE].astype(v_ref.dtype)


def _prep_mla_weights(w_uq, w_ukv):
    H = MLA_HEADS
    half = MLA_ROPE // 2
    wq = w_uq.reshape(MLA_Q_LORA, H, MLA_NOPE + MLA_ROPE)
    nope, rope = wq[..., :MLA_NOPE], wq[..., MLA_NOPE:]
    rot = jnp.concatenate([-rope[..., half:], rope[..., :half]], axis=-1)
    zq = lambda n: jnp.zeros((MLA_Q_LORA, H, n), w_uq.dtype)
    pad = QK_W - MLA_NOPE - MLA_ROPE
    wq_main = jnp.concatenate([nope, rope, zq(pad)], axis=-1).reshape(MLA_Q_LORA, H * QK_W)
    wq_rot = jnp.concatenate([zq(MLA_NOPE), rot, zq(pad)], axis=-1).reshape(MLA_Q_LORA, H * QK_W)
    wkv = w_ukv.reshape(MLA_KV_LORA, H, MLA_NOPE + MLA_V)
    wk = jnp.concatenate([wkv[..., :MLA_NOPE], jnp.zeros((MLA_KV_LORA, H, QK_W - MLA_NOPE), w_ukv.dtype)],
                         axis=-1).reshape(MLA_KV_LORA, H * QK_W)
    wv = wkv[..., MLA_NOPE:].reshape(MLA_KV_LORA, H * MLA_V)
    return wq_main.astype(BF16), wq_rot.astype(BF16), wk.astype(BF16), wv.astype(BF16)


def _attn_kernel(q_ref, k_ref, v_ref, o_ref, acc_ref, s_ref, *, tq):
    qi = pl.program_id(2)
    lane = lax.broadcasted_iota(jnp.int32, (1, LANE), 1)
    first = lane < MLA_V
    acc_ref[...] = jnp.zeros_like(acc_ref)

    def scores(j, slot):
        rows = pl.ds(pl.multiple_of(j * tq, tq), tq)
        for e in range(2):
            s_ref[slot, e] = _dot_nt(q_ref[e], k_ref[e, rows, :])

    def consume(j, slot, carry, masked):
        m0, l0, m1, l1 = carry
        rows = pl.ds(pl.multiple_of(j * tq, tq), tq)
        v = v_ref[rows, :]
        zero = jnp.zeros_like(v)
        outs = []
        for e, m, l in ((0, m0, l0), (1, m1, l1)):
            s = s_ref[slot, e]
            if masked:
                r = lax.broadcasted_iota(jnp.int32, s.shape, 0)
                c = lax.broadcasted_iota(jnp.int32, s.shape, 1)
                s = jnp.where(c <= r, s, -jnp.inf)
            m_new = jnp.maximum(m, jnp.max(s, axis=-1, keepdims=True))
            p = jnp.exp(s - m_new)
            alpha = jnp.exp(m - m_new)
            l_new = alpha * l + jnp.sum(p, axis=-1, keepdims=True)
            ve = jnp.where(first, v, zero) if e == 0 else jnp.where(first, zero, v)
            outs.append((m_new, l_new, alpha, _dot(p.astype(v.dtype), ve)))
        (m0, l0, a0, pv0), (m1, l1, a1, pv1) = outs
        acc_ref[...] = acc_ref[...] * jnp.where(first, a0, a1) + (pv0 + pv1)
        return m0, l0, m1, l1

    def pair(i, carry):
        scores(2 * i + 1, 1)
        carry = consume(2 * i, 0, carry, False)
        scores(2 * i + 2, 0)
        return consume(2 * i + 1, 1, carry, False)

    def even_tail(carry):
        return consume(qi, 0, carry, True)

    def odd_tail(carry):
        scores(qi, 1)
        return consume(qi, 1, consume(qi - 1, 0, carry, False), True)

    neg = jnp.full((tq, 1), -jnp.inf, F32)
    zer = jnp.zeros((tq, 1), F32)
    scores(0, 0)
    carry = lax.fori_loop(0, qi // 2, pair, (neg, zer, neg, zer))
    _, l0, _, l1 = lax.cond(qi % 2 == 0, even_tail, odd_tail, carry)
    o_ref[...] = (acc_ref[...] / jnp.where(first, l0, l1)).astype(o_ref.dtype)


def _mla_attn(q, k, v):
    B, H, S, _ = q.shape
    tq = _tile(S, TILE_ROWS["attn"])
    return pl.pallas_call(
        functools.partial(_attn_kernel, tq=tq),
        grid=(B, H // 2, S // tq),
        in_specs=[pl.BlockSpec((None, 2, tq, QK_W), lambda b, h, i: (b, h, i, 0)),
                  pl.BlockSpec((None, 2, S, QK_W), lambda b, h, i: (b, h, 0, 0)),
                  pl.BlockSpec((None, None, S, LANE), lambda b, h, i: (b, h, 0, 0))],
        out_specs=pl.BlockSpec((None, tq, LANE), lambda b, h, i: (b, i, h)),
        out_shape=jax.ShapeDtypeStruct((B, S, BRANCH_W), BRANCH_DTYPE),
        scratch_shapes=[pltpu.VMEM((tq, LANE), F32), pltpu.VMEM((2, 2, tq, tq), F32)],
        compiler_params=pltpu.CompilerParams(dimension_semantics=("parallel", "parallel", "arbitrary"),
                                             vmem_limit_bytes=VMEM_LIMIT),
        name="mla_attn",
    )(q, k, v)


def _expand(x, bd16):
    x16 = x.astype(BF16)
    C, G = x.shape
    zero = jnp.zeros((C, LANE), BF16)
    blocks = []
    for h in range(bd16.shape[0] // C):
        lo = (h * RWKV_HEAD // LANE) * LANE
        own = x16[:, lo:lo + LANE] * bd16[h * C:(h + 1) * C, lo:lo + LANE]
        blocks.append(jnp.concatenate([own if l0 == lo else zero for l0 in range(0, G, LANE)], axis=1))
    return jnp.concatenate(blocks, axis=0)


def _tri_inverse(ns, eye, bd16):
    C = ns[0].shape[0]
    ps = [eye + n for n in ns]
    ss = [_dot(n.astype(BF16), _expand(n, bd16)) for n in ns]
    for _ in range(C.bit_length() - 3):
        sps = [_dot(jnp.concatenate([s, p], axis=0).astype(BF16), _expand(s, bd16)) for s, p in zip(ss, ps)]
        ss = [sp[:C] for sp in sps]
        ps = [p + sp[C:] for p, sp in zip(ps, sps)]
    return [p + _dot(p.astype(BF16), _expand(s, bd16)) for s, p in zip(ss, ps)]


def _rwkv_kernel(u_ref, mu_ref, w0_ref, wup_ref, a0_ref, aup_ref, kk_ref, ka_ref, rk_ref, lnw_ref, lnb_ref,
                 bd_ref, bd16_ref, o_ref, state_ref, last_ref):
    C = RWKV_CHUNK
    W = BRANCH_W
    G = RWKV_QUAD
    NB = u_ref.shape[0]
    R = NB * C
    quads = [slice(q * G, (q + 1) * G) for q in range(W // G)]
    seqs = [slice(e * C, (e + 1) * C) for e in range(NB)]

    @pl.when(pl.program_id(1) == 0)
    def _():
        state_ref[...] = jnp.zeros_like(state_ref)
        last_ref[...] = jnp.zeros_like(last_ref)

    u = u_ref[...].reshape(R, RWKV_IN)
    row = lax.broadcasted_iota(jnp.int32, (R, 1), 0)
    prev = pltpu.roll(u, 1, axis=0)
    for e in range(NB):
        prev = jnp.where(row == e * C, last_ref[e], prev)
        last_ref[e] = u[(e + 1) * C - 1:(e + 1) * C, :]
    u = u + (prev - u) * mu_ref[...]
    r, k, v = u[:, 0:W], u[:, W:2 * W], u[:, 2 * W:3 * W]
    lwa = u[:, 3 * W:]
    bd16 = bd16_ref[...]

    def segsum(x):
        return jnp.concatenate([_dot_exact_rhs(x[:, sl], bd16, parts=2) for sl in quads], axis=1)

    w = -jax.nn.softplus(-(w0_ref[...] + _dot_hi(jnp.tanh(lwa), wup_ref[...]))) - 0.5
    logd = -jnp.exp(w)
    a = jax.nn.sigmoid(a0_ref[...] + _dot_hi(lwa, aup_ref[...]))
    kk = k * kk_ref[...]
    kk = kk / jnp.maximum(jnp.sqrt(segsum(kk * kk)), 1e-12)
    k = k * (1.0 + (a - 1.0) * ka_ref[...])
    bonus = segsum(r * k * rk_ref[...]) * v

    tr = lax.broadcasted_iota(jnp.int32, (R, R), 0)
    tc = lax.broadcasted_iota(jnp.int32, (R, R), 1)
    tri = jnp.where((tc <= tr) & (tc >= (tr // C) * C), 1.0, 0.0).astype(BF16)
    b = _dot_exact_lhs(tri, logd)
    btot = jnp.concatenate([jnp.broadcast_to(b[sq.stop - 1:sq.stop, :], (C, W)) for sq in seqs], axis=0)
    enb = jnp.exp(-b)
    rt = r * jnp.exp(b)
    alt = -kk * jnp.exp(b - logd)
    beta = kk * a
    bt = beta * enb
    kt = k * enb
    edec = jnp.exp(btot - b)
    bhat = beta * edec
    khat = k * edec
    gams = [jnp.exp(b[sq.stop - 1:sq.stop, :]) for sq in seqs]

    t_idx = lax.broadcasted_iota(jnp.int32, (C, G), 0)
    s_idx = lax.broadcasted_iota(jnp.int32, (C, G), 1) % C
    incl = s_idx <= t_idx
    strict = s_idx < t_idx
    eye = jnp.where(s_idx == t_idx, 1.0, 0.0)

    chains = [(e, q, sq, sl) for e, sq in enumerate(seqs) for q, sl in enumerate(quads)]
    ars = [jnp.concatenate([alt[sq, sl], rt[sq, sl]], axis=0).astype(BF16) for _, _, sq, sl in chains]
    abs_ = [_dot_nt(ar, _expand(bt[sq, sl], bd16)) for ar, (_, _, sq, sl) in zip(ars, chains)]
    aks = [_dot_nt(ar, _expand(kt[sq, sl], bd16)) for ar, (_, _, sq, sl) in zip(ars, chains)]
    t16s = [t.astype(BF16) for t in _tri_inverse([jnp.where(strict, ab[:C], 0.0) for ab in abs_], eye, bd16)]
    m16s = [_dot(t16, _expand(jnp.where(strict, ak[:C], 0.0), bd16)).astype(BF16) for t16, ak in zip(t16s, aks)]
    a_rbs = [jnp.where(incl, ab[C:], 0.0).astype(BF16) for ab in abs_]
    a_rks = [jnp.where(incl, ak[C:], 0.0).astype(BF16) for ak in aks]
    evs = [_expand(v[sq, sl], bd16) for _, _, sq, sl in chains]
    hts = [state_ref[e, q] for e, q, _, _ in chains]
    prs = [_dot_nt(ar, ht.astype(BF16)) for ar, ht in zip(ars, hts)]
    us = [_dot(t16, _expand(pr[:C], bd16)) + _dot(m16, ev) for t16, pr, m16, ev in zip(t16s, prs, m16s, evs)]
    for (e, q, sq, sl), ht, uq in zip(chains, hts, us):
        z = jnp.concatenate([v[sq, sl], uq], axis=0).astype(BF16)
        wk = jnp.concatenate([khat[sq, sl], bhat[sq, sl]], axis=0).astype(BF16)
        state_ref[e, q] = ht * gams[e][:, sl] + bd_ref[...] * _dot_tn(z, wk)
    ys = [pr[C:] + _dot(a_rb, _expand(uq, bd16)) + _dot(a_rk, ev)
          for pr, a_rb, uq, a_rk, ev in zip(prs, a_rbs, us, a_rks, evs)]
    nq = len(quads)
    y = jnp.concatenate([jnp.concatenate(ys[e * nq:(e + 1) * nq], axis=1) for e in range(NB)], axis=0)
    mean = segsum(y) * (1.0 / RWKV_HEAD)
    d = y - mean
    var = segsum(d * d) * (1.0 / RWKV_HEAD)
    y = d * lax.rsqrt(var + RWKV_GN_EPS)
    o_ref[...] = (y * lnw_ref[...] + lnb_ref[...] + bonus).reshape(NB, C, W).astype(o_ref.dtype)


def _rwkv(p_rwkv, mu, w0, w_up, a0, a_up, k_k, k_a, r_k, ln_w, ln_b):
    B, S, _ = p_rwkv.shape
    C, W, G = RWKV_CHUNK, BRANCH_W, RWKV_QUAD
    NB = _tile(B, RWKV_SEQS)
    assert C == RWKV_HEAD
    wup = jnp.concatenate([w_up, jnp.zeros_like(a_up)], axis=0)
    aup = jnp.concatenate([jnp.zeros_like(w_up), a_up], axis=0)
    bd = _block_diag_ones(G, RWKV_HEAD)
    row = lambda n: pl.BlockSpec((1, n), lambda b, c: (0, 0))
    full = lambda a: pl.BlockSpec(a.shape, lambda b, c: (0,) * a.ndim)
    return pl.pallas_call(
        _rwkv_kernel,
        grid=(B // NB, S // C),
        in_specs=[pl.BlockSpec((NB, C, RWKV_IN), lambda b, c: (b, c, 0)),
                  row(RWKV_IN), row(W), full(wup), row(W), full(aup), row(W), row(W), row(W), row(W), row(W),
                  full(bd), full(bd)],
        out_specs=pl.BlockSpec((NB, C, W), lambda b, c: (b, c, 0)),
        out_shape=jax.ShapeDtypeStruct((B, S, W), BRANCH_DTYPE),
        scratch_shapes=[pltpu.VMEM((NB, W // G, G, G), F32), pltpu.VMEM((NB, 1, RWKV_IN), F32)],
        compiler_params=pltpu.CompilerParams(dimension_semantics=("parallel", "arbitrary"),
                                             vmem_limit_bytes=VMEM_LIMIT),
        name="rwkv7",
    )(p_rwkv, mu, w0, wup, a0, aup, k_k, k_a, r_k.reshape(1, W), ln_w, ln_b, bd, bd.astype(BF16))


def _gla_kernel(p_ref, aup_ref, ab_ref, gn_ref, seg_ref, bdk_ref, bd_ref, o_ref, state_ref, *, nchunk):
    C = GLA_CHUNK
    c = GLA_BLOCK
    nb = C // c
    KW = GLA_HEADS * GLA_DK
    VW = GLA_HEADS * GLA_DV
    reps = GLA_HEADS

    @pl.when(pl.program_id(1) == 0)
    def _():
        state_ref[...] = jnp.zeros_like(state_ref)

    tr = lax.broadcasted_iota(jnp.int32, (C, C), 0)
    tc = lax.broadcasted_iota(jnp.int32, (C, C), 1)
    tri = jnp.where(tc <= tr, 1.0, 0.0).astype(BF16)
    row_c = lax.broadcasted_iota(jnp.int32, (c, 1), 0)
    row_C = lax.broadcasted_iota(jnp.int32, (C, 1), 0)

    def chunk(i, carry):
        rows = pl.ds(pl.multiple_of(i * C, C), C)
        q = p_ref[rows, 0:KW] * (GLA_DK ** -0.5)
        k = p_ref[rows, KW:2 * KW]
        v = p_ref[rows, 2 * KW:2 * KW + VW]
        lat = p_ref[rows, 2 * KW + VW:]
        z = _dot_hi(lat, aup_ref[...]) + ab_ref[...]
        log_a = jax.nn.log_sigmoid(z) * (1.0 / GLA_TAU)
        b = _dot_exact_lhs(tri, log_a)
        ht = state_ref[...]
        o_inter = _dot_nt((q * jnp.exp(b)).astype(BF16), ht.astype(BF16))

        v16 = v.astype(BF16)
        ev = jnp.concatenate([v16] * reps, axis=0) * seg_ref[...]
        atts = []
        for blk in range(1, nb):
            lo = blk * c
            m = b[lo - 1:lo, :]
            qb = q[lo:lo + c] * jnp.exp(b[lo:lo + c] - m)
            kb = (k * jnp.exp(jnp.where(row_C < lo, m - b, -jnp.inf))).astype(BF16)
            ke = jnp.concatenate([kb] * reps, axis=0) * bdk_ref[...]
            atts.append(_dot_nt(qb.astype(BF16), ke))
        o_off = _dot(jnp.concatenate(atts, axis=0).astype(BF16), ev)

        outs = []
        for blk in range(nb):
            lo = blk * c
            bb, qq, kk, vv = b[lo:lo + c], q[lo:lo + c], k[lo:lo + c], v[lo:lo + c]
            xs = []
            for j in range(c):
                diff = jnp.where(row_c >= j, bb - bb[j:j + 1, :], -jnp.inf)
                xs.append(qq * jnp.exp(diff) * kk[j:j + 1, :])
            att = _dot(jnp.concatenate(xs, axis=0).astype(BF16), seg_ref[...])
            o = o_inter[lo:lo + c]
            if blk:
                o = o + o_off[lo - c:lo]
            for j in range(c):
                o = o + att[j * c:(j + 1) * c, :] * vv[j:j + 1, :]
            outs.append(o)
        o = jnp.concatenate(outs, axis=0)

        blast = b[C - 1:C, :]
        khat = (k * jnp.exp(blast - b)).astype(BF16)
        state_ref[...] = ht * jnp.exp(blast) + bd_ref[...] * _dot_tn(v16, khat)
        heads = []
        for h in range(GLA_HEADS):
            hs = slice(h * GLA_DV, (h + 1) * GLA_DV)
            heads.append(_rms(o[:, hs], gn_ref[:, hs]))
        o_ref[rows, :] = jnp.concatenate(heads, axis=1).astype(o_ref.dtype)
        return carry

    lax.fori_loop(0, nchunk, chunk, 0)


def _gla(p_gla, a_up, a_b, g_norm):
    B, S, _ = p_gla.shape
    KW = GLA_HEADS * GLA_DK
    VW = GLA_HEADS * GLA_DV
    tg = _tile(S, TILE_ROWS["gla"])
    assert GLA_CHUNK == GLA_DK and tg % GLA_CHUNK == 0
    aup = jnp.concatenate([a_up, jnp.zeros((GLA_IN_W - 2 * KW - VW - GLA_GATE_RANK, KW), a_up.dtype)], axis=0)
    seg = _block_diag_ones(KW, GLA_DK, GLA_DV, VW, dtype=BF16)
    bdk = _block_diag_ones(KW, GLA_DK, dtype=BF16)
    bd = _block_diag_ones(VW, GLA_DV, GLA_DK, KW)
    full = lambda a: pl.BlockSpec(a.shape, lambda b, t: (0,) * a.ndim)
    return pl.pallas_call(
        functools.partial(_gla_kernel, nchunk=tg // GLA_CHUNK),
        grid=(B, S // tg),
        in_specs=[pl.BlockSpec((None, tg, GLA_IN_W), lambda b, t: (b, t, 0)),
                  full(aup), pl.BlockSpec((1, KW), lambda b, t: (0, 0)), pl.BlockSpec((1, VW), lambda b, t: (0, 0)),
                  full(seg), full(bdk), full(bd)],
        out_specs=pl.BlockSpec((None, tg, VW), lambda b, t: (b, t, 0)),
        out_shape=jax.ShapeDtypeStruct((B, S, VW), BRANCH_DTYPE),
        scratch_shapes=[pltpu.VMEM((VW, KW), F32)],
        compiler_params=pltpu.CompilerParams(dimension_semantics=("parallel", "arbitrary"),
                                             vmem_limit_bytes=VMEM_LIMIT),
        name="gla",
    )(p_gla, aup, a_b, g_norm, seg, bdk, bd)


def _merge_kernel(x_ref, ym_ref, yr_ref, yg_ref, gate_ref, mg_ref, wb_ref, wo_ref, gpost_ref, o_ref):
    W = BRANCH_W
    merged = None
    for n, y_ref in enumerate((ym_ref, yr_ref, yg_ref)):
        ys = (y_ref[...].astype(F32) * jax.nn.silu(gate_ref[:, n * W:(n + 1) * W].astype(F32))).astype(BF16)
        br = _dot(ys, wb_ref[n]) * jax.nn.sigmoid(mg_ref[:, n * D_MODEL:(n + 1) * D_MODEL].astype(F32))
        merged = br if merged is None else merged + br
    out = _dot(merged.astype(BF16), wo_ref[...])
    o_ref[...] = x_ref[...] + _rms(out, gpost_ref[...])


def _merge(x2, ym, yr, yg, gate, mg, wb, wo, gpost):
    N = x2.shape[0]
    W = BRANCH_W
    tm = _tile(N, TILE_ROWS["merge"])
    tile = lambda n: pl.BlockSpec((tm, n), lambda i: (i, 0))
    return pl.pallas_call(
        _merge_kernel,
        grid=(N // tm,),
        in_specs=[tile(D_MODEL), tile(W), tile(W), tile(W), tile(GATE_W), tile(MERGE_W),
                  pl.BlockSpec((N_BRANCH, W, D_MODEL), lambda i: (0, 0, 0)),
                  pl.BlockSpec((D_MODEL, D_MODEL), lambda i: (0, 0)),
                  pl.BlockSpec((1, D_MODEL), lambda i: (0, 0))],
        out_specs=tile(D_MODEL),
        out_shape=jax.ShapeDtypeStruct((N, D_MODEL), F32),
        compiler_params=pltpu.CompilerParams(dimension_semantics=("parallel",), vmem_limit_bytes=VMEM_LIMIT),
        name="merge_out",
    )(x2, ym, yr, yg, gate, mg, wb, wo, gpost)


def kernel(x, positions, norm_pre, w_in, mla_q_norm, mla_kv_norm, mla_w_uq, mla_w_ukv, rwkv_mu, rwkv_w0,
           rwkv_w_up, rwkv_a0, rwkv_a_up, rwkv_k_k, rwkv_k_a, rwkv_r_k, rwkv_ln_w, rwkv_ln_b, gla_a_up, gla_a_b,
           gla_norm, w_branch_out, w_out, norm_post):
    B, S, D = x.shape
    assert D == D_MODEL and S % RWKV_CHUNK == 0
    depth = w_in.shape[0]
    N = B * S
    cosm, sinm = _rope_tables(positions)
    x2 = x.reshape(N, D)
    row = lambda a: a.reshape(1, -1)
    for l in range(depth):
        wq, wqr, wk, wv = _prep_mla_weights(mla_w_uq[l], mla_w_ukv[l])
        q, k, v, p_rwkv, p_gla, p_gate, p_merge = _inproj(
            x2, row(norm_pre[l]), _prep_w_in(w_in[l]), cosm, sinm, row(mla_q_norm[l]), row(mla_kv_norm[l]),
            wq, wqr, wk, wv, B, S)
        y_mla = _mla_attn(q, k, v)
        y_rwkv = _rwkv(p_rwkv.reshape(B, S, -1), row(rwkv_mu[l]), row(rwkv_w0[l]), rwkv_w_up[l], row(rwkv_a0[l]),
                       rwkv_a_up[l], row(rwkv_k_k[l]), row(rwkv_k_a[l]), rwkv_r_k[l], row(rwkv_ln_w[l]),
                       row(rwkv_ln_b[l]))
        y_gla = _gla(p_gla.reshape(B, S, -1), gla_a_up[l], row(gla_a_b[l]), row(gla_norm[l]))
        x2 = _merge(x2, y_mla.reshape(N, -1), y_rwkv.reshape(N, -1), y_gla.reshape(N, -1), p_gate, p_merge,
                    w_branch_out[l].astype(BF16), w_out[l].astype(BF16), row(norm_post[l]))
    return x2.reshape(B, S, D)
```
